```python
import math
import jax, jax.numpy as jnp
from jax import lax
import numpy as np

D_MODEL = 1024
BATCH = 8
SEQ = 2048
DEPTH = 2

GRID_W = 64
CTX_LEN = 256
N_MIXERS = 2
DIFF_HEAD_DIM = 64
DIFF_HEADS = D_MODEL // (2 * DIFF_HEAD_DIM)
GQA_HEAD_DIM = 128
GQA_Q_HEADS = D_MODEL // GQA_HEAD_DIM
GQA_KV_HEADS = 2
GQA_GROUP = GQA_Q_HEADS // GQA_KV_HEADS
Q_BLOCK = 128
ROPE_THETA = 10000.0
N_GROUPS = 4
EXPERTS_PER_GROUP = 8
N_EXPERTS = N_GROUPS * EXPERTS_PER_GROUP
EXPERT_TOP_K = 2
EXPERT_HIDDEN = D_MODEL // 4
N_MOD = 6
LN_EPS = 1e-5
RMS_EPS = 1e-6
DEEPNORM_ALPHA = (2 * DEPTH) ** 0.25
DEEPNORM_BETA = (8 * DEPTH) ** -0.25
N_DIFF_LAYERS = (DEPTH + 1) // 2
N_GQA_LAYERS = DEPTH // 2

kernel_name = "hybrid_diffattn_gqa_hmoe_prefix_dit"


def layer_norm(x, g, b):
    xf = x.astype(jnp.float32)
    mu = jnp.mean(xf, axis=-1, keepdims=True)
    var = jnp.mean(jnp.square(xf - mu), axis=-1, keepdims=True)
    return ((xf - mu) * lax.rsqrt(var + LN_EPS) * g + b).astype(x.dtype)


def rms_norm(x, g):
    xf = x.astype(jnp.float32)
    return (xf * lax.rsqrt(jnp.mean(xf * xf, axis=-1, keepdims=True) + RMS_EPS) * g).astype(x.dtype)


def modulation(cond, w_mod, b_mod):
    return jnp.split(jax.nn.silu(cond) @ w_mod + b_mod, N_MOD, axis=-1)


def axial_rope_tables(n, head_dim):
    rows = n // GRID_W
    row = jnp.broadcast_to(jnp.arange(rows, dtype=jnp.float32)[:, None], (rows, GRID_W)).reshape(-1)
    col = jnp.broadcast_to(jnp.arange(GRID_W, dtype=jnp.float32)[None, :], (rows, GRID_W)).reshape(-1)
    axis_dim = head_dim // 2
    inv_freq = ROPE_THETA ** (-jnp.arange(0, axis_dim, 2, dtype=jnp.float32) / axis_dim)
    ang = jnp.stack([row, col], axis=-1)[:, :, None] * inv_freq
    return jnp.cos(ang), jnp.sin(ang)


def apply_rope(x, cos, sin):
    d = x.shape[-1]
    xs = x.reshape(*x.shape[:-1], 2, 2, d // 4)
    x1, x2 = xs[..., 0, :], xs[..., 1, :]
    cos = cos.astype(x.dtype)
    sin = sin.astype(x.dtype)
    out = jnp.stack([x1 * cos - x2 * sin, x2 * cos + x1 * sin], axis=-2)
    return out.reshape(x.shape)


def sweep_queries(attend, q):
    n, dq = q.shape[-2], q.shape[-1]
    nb = n // Q_BLOCK
    qb = jnp.moveaxis(q.reshape(*q.shape[:-2], nb, Q_BLOCK, dq), -3, 0)
    ob = jnp.moveaxis(lax.map(attend, qb), 0, -3)
    return ob.reshape(*ob.shape[:-3], n, ob.shape[-1])


def diff_attention(hc, hx, w_qkv, w_o, lq1, lk1, lq2, lk2, subln_g, lambda_init, need_ctx):
    B, L, _ = hc.shape
    N = hx.shape[1]
    H, d = DIFF_HEADS, DIFF_HEAD_DIM
    T = L + N
    qkv = jnp.concatenate([hc, hx], axis=1) @ w_qkv
    q, k, v = jnp.split(qkv, 3, axis=-1)
    q = q.reshape(B, T, 2 * H, d).transpose(0, 2, 1, 3)
    k = k.reshape(B, T, 2 * H, d).transpose(0, 2, 1, 3)
    v = v.reshape(B, T, H, 2 * d).transpose(0, 2, 1, 3)
    cos, sin = axial_rope_tables(N, d)
    qx = apply_rope(q[:, :, L:], cos, sin)
    kx = apply_rope(k[:, :, L:], cos, sin)
    kc, vc = k[:, :, :L], v[:, :, :L]
    k_all = jnp.concatenate([kc, kx], axis=2)
    lam = (jnp.exp(jnp.sum(lq1.astype(jnp.float32) * lk1.astype(jnp.float32)))
           - jnp.exp(jnp.sum(lq2.astype(jnp.float32) * lk2.astype(jnp.float32))) + lambda_init)
    scale = d ** -0.5

    def diff_map(qb, keys, vals):
        s = jnp.einsum('bhqd,bhkd->bhqk', qb, keys).astype(jnp.float32) * scale
        p = jax.nn.softmax(s, axis=-1)
        p = p.reshape(B, H, 2, *p.shape[2:])
        a = p[:, :, 0] - lam * p[:, :, 1]
        return jnp.einsum('bhqk,bhkd->bhqd', a.astype(vals.dtype), vals)

    def finish(o):
        o = rms_norm(o, subln_g) * (1.0 - lambda_init)
        return o.transpose(0, 2, 1, 3).reshape(B, o.shape[2], H * 2 * d) @ w_o

    ox = finish(sweep_queries(lambda qb: diff_map(qb, k_all, v), qx))
    oc = finish(sweep_queries(lambda qb: diff_map(qb, kc, vc), q[:, :, :L])) if need_ctx else None
    return ox, oc


def gqa_attention(hc, hx, w_qkv, w_o, q_norm_g, k_norm_g, need_ctx):
    B, L, _ = hc.shape
    N = hx.shape[1]
    Hk, G, d = GQA_KV_HEADS, GQA_GROUP, GQA_HEAD_DIM
    T = L + N
    qkv = jnp.concatenate([hc, hx], axis=1) @ w_qkv
    q, k, v = jnp.split(qkv, [GQA_Q_HEADS * d, GQA_Q_HEADS * d + Hk * d], axis=-1)
    q = rms_norm(q.reshape(B, T, Hk, G, d), q_norm_g).transpose(0, 2, 3, 1, 4)
    k = rms_norm(k.reshape(B, T, Hk, d), k_norm_g).transpose(0, 2, 1, 3)
    v = v.reshape(B, T, Hk, d).transpose(0, 2, 1, 3)
    cos, sin = axial_rope_tables(N, d)
    qx = apply_rope(q[:, :, :, L:], cos, sin)
    kx = apply_rope(k[:, :, L:], cos, sin)
    kc, vc = k[:, :, :L], v[:, :, :L]
    k_all = jnp.concatenate([kc, kx], axis=2)
    scale = d ** -0.5

    def gqa_map(qb, keys, vals):
        s = jnp.einsum('bhgqd,bhkd->bhgqk', qb, keys).astype(jnp.float32) * scale
        p = jax.nn.softmax(s, axis=-1)
        return jnp.einsum('bhgqk,bhkd->bhgqd', p.astype(vals.dtype), vals)

    def finish(o):
        return o.transpose(0, 3, 1, 2, 4).reshape(B, o.shape[3], GQA_Q_HEADS * d) @ w_o

    ox = finish(sweep_queries(lambda qb: gqa_map(qb, k_all, v), qx))
    oc = finish(sweep_queries(lambda qb: gqa_map(qb, kc, vc), q[:, :, :, :L])) if need_ctx else None
    return ox, oc


def hier_moe(h, w_group, b_group, w_router, b_router, w_gate, w_up, w_down):
    shp = h.shape
    hf = h.reshape(-1, shp[-1])
    g_prob = jax.nn.softmax((hf @ w_group + b_group).astype(jnp.float32), axis=-1)
    g_top, g_idx = lax.top_k(g_prob, 1)
    g_onehot = jax.nn.one_hot(g_idx[:, 0], N_GROUPS, dtype=jnp.float32)
    e_logits = (hf @ w_router + b_router).astype(jnp.float32).reshape(-1, N_GROUPS, EXPERTS_PER_GROUP)
    e_sel = jnp.einsum('mg,mge->me', g_onehot, e_logits)
    e_prob = jax.nn.softmax(e_sel, axis=-1)
    e_top, e_idx = lax.top_k(e_prob, EXPERT_TOP_K)
    e_top = e_top / jnp.sum(e_top, axis=-1, keepdims=True)
    w_within = jnp.einsum('mk,mke->me', e_top, jax.nn.one_hot(e_idx, EXPERTS_PER_GROUP, dtype=jnp.float32))
    gate = g_onehot[:, :, None] * (g_top * w_within)[:, None, :]
    y = jnp.zeros_like(hf)
    for g in range(N_GROUPS):
        a = jnp.einsum('md,edf->mef', hf, w_gate[g])
        u = jnp.einsum('md,edf->mef', hf, w_up[g])
        act = jax.nn.silu(a) * u * gate[:, g, :, None].astype(hf.dtype)
        y = y + jnp.einsum('mef,efd->md', act, w_down[g])
    return y.reshape(shp)


def setup_inputs(seed: int = 0) -> dict:
    key = jax.random.key(seed)
    ks = jax.random.split(key, 32)
    D, F = D_MODEL, EXPERT_HIDDEN

    def nrm(k, shape, scale):
        return jax.random.normal(k, shape, jnp.float32) * scale

    gqa_qkv_width = GQA_Q_HEADS * GQA_HEAD_DIM + 2 * GQA_KV_HEADS * GQA_HEAD_DIM
    return {
        "x": nrm(ks[0], (BATCH, SEQ, D), 1.0),
        "c": nrm(ks[1], (BATCH, D), 1.0),
        "ctx": nrm(ks[2], (BATCH, CTX_LEN, D), 1.0),
        "c_ctx": nrm(ks[3], (D,), 1.0),
        "w_mod": nrm(ks[4], (DEPTH, D, N_MOD * D), 0.5 * D ** -0.5),
        "b_mod": nrm(ks[5], (DEPTH, N_MOD * D), 0.02),
        "ln_mix_g": 1.0 + nrm(ks[6], (DEPTH, D), 0.02),
        "ln_mix_b": nrm(ks[7], (DEPTH, D), 0.02),
        "ln_ffn_g": 1.0 + nrm(ks[8], (DEPTH, D), 0.02),
        "ln_ffn_b": nrm(ks[9], (DEPTH, D), 0.02),
        "diff_w_qkv": nrm(ks[10], (N_DIFF_LAYERS, D, 3 * D), D ** -0.5),
        "diff_w_o": nrm(ks[11], (N_DIFF_LAYERS, D, D), D ** -0.5 * DEEPNORM_BETA),
        "diff_lambda_q1": nrm(ks[12], (N_DIFF_LAYERS, DIFF_HEAD_DIM), 0.1),
        "diff_lambda_k1": nrm(ks[13], (N_DIFF_LAYERS, DIFF_HEAD_DIM), 0.1),
        "diff_lambda_q2": nrm(ks[14], (N_DIFF_LAYERS, DIFF_HEAD_DIM), 0.1),
        "diff_lambda_k2": nrm(ks[15], (N_DIFF_LAYERS, DIFF_HEAD_DIM), 0.1),
        "diff_subln_g": 1.0 + nrm(ks[16], (N_DIFF_LAYERS, 2 * DIFF_HEAD_DIM), 0.02),
        "gqa_w_qkv": nrm(ks[17], (N_GQA_LAYERS, D, gqa_qkv_width), D ** -0.5),
        "gqa_w_o": nrm(ks[18], (N_GQA_LAYERS, D, D), D ** -0.5 * DEEPNORM_BETA),
        "gqa_q_norm_g": 1.0 + nrm(ks[19], (N_GQA_LAYERS, GQA_HEAD_DIM), 0.02),
        "gqa_k_norm_g": 1.0 + nrm(ks[20], (N_GQA_LAYERS, GQA_HEAD_DIM), 0.02),
        "moe_w_group": nrm(ks[21], (DEPTH, D, N_GROUPS), D ** -0.5),
        "moe_b_group": nrm(ks[22], (DEPTH, N_GROUPS), 0.01),
        "moe_w_router": nrm(ks[23], (DEPTH, D, N_EXPERTS), D ** -0.5),
        "moe_b_router": nrm(ks[24], (DEPTH, N_EXPERTS), 0.01),
        "moe_w_gate": nrm(ks[25], (DEPTH, N_GROUPS, EXPERTS_PER_GROUP, D, F), D ** -0.5),
        "moe_w_up": nrm(ks[26], (DEPTH, N_GROUPS, EXPERTS_PER_GROUP, D, F), D ** -0.5),
        "moe_w_down": nrm(ks[27], (DEPTH, N_GROUPS, EXPERTS_PER_GROUP, F, D), F ** -0.5 * DEEPNORM_BETA),
    }


def reference(x, c, ctx, c_ctx, w_mod, b_mod, ln_mix_g, ln_mix_b, ln_ffn_g, ln_ffn_b,
              diff_w_qkv, diff_w_o, diff_lambda_q1, diff_lambda_k1, diff_lambda_q2, diff_lambda_k2,
              diff_subln_g, gqa_w_qkv, gqa_w_o, gqa_q_norm_g, gqa_k_norm_g,
              moe_w_group, moe_b_group, moe_w_router, moe_b_router, moe_w_gate, moe_w_up, moe_w_down):
    L = ctx.shape[1]
    for i in range(DEPTH):
        need_ctx = i < DEPTH - 1
        sx_m, ax_m, gx_m, sx_f, ax_f, gx_f = modulation(c[:, None, :], w_mod[i], b_mod[i])
        sc_m, ac_m, gc_m, sc_f, ac_f, gc_f = modulation(c_ctx, w_mod[i], b_mod[i])

        hx = x * (1.0 + ax_m) + sx_m
        hc = ctx * (1.0 + ac_m) + sc_m
        j = i // N_MIXERS
        if i % N_MIXERS == 0:
            lambda_init = 0.8 - 0.6 * math.exp(-0.3 * i)
            ox, oc = diff_attention(hc, hx, diff_w_qkv[j], diff_w_o[j], diff_lambda_q1[j], diff_lambda_k1[j],
                                    diff_lambda_q2[j], diff_lambda_k2[j], diff_subln_g[j], lambda_init, need_ctx)
        else:
            ox, oc = gqa_attention(hc, hx, gqa_w_qkv[j], gqa_w_o[j], gqa_q_norm_g[j], gqa_k_norm_g[j], need_ctx)
        x = layer_norm(DEEPNORM_ALPHA * x + gx_m * ox, ln_mix_g[i], ln_mix_b[i])

        hx = x * (1.0 + ax_f) + sx_f
        moe_args = (moe_w_group[i], moe_b_group[i], moe_w_router[i], moe_b_router[i],
                    moe_w_gate[i], moe_w_up[i], moe_w_down[i])
        if need_ctx:
            ctx = layer_norm(DEEPNORM_ALPHA * ctx + gc_m * oc, ln_mix_g[i], ln_mix_b[i])
            hc = ctx * (1.0 + ac_f) + sc_f
            f = hier_moe(jnp.concatenate([hc, hx], axis=1), *moe_args)
            fc, fx = f[:, :L], f[:, L:]
            ctx = layer_norm(DEEPNORM_ALPHA * ctx + gc_f * fc, ln_ffn_g[i], ln_ffn_b[i])
        else:
            fx = hier_moe(hx, *moe_args)
        x = layer_norm(DEEPNORM_ALPHA * x + gx_f * fx, ln_ffn_g[i], ln_ffn_b[i])
    return x
```

```python
import functools
import math

import jax
import jax.numpy as jnp
from jax import lax
from jax.experimental import pallas as pl
from jax.experimental.pallas import tpu as pltpu

F32 = jnp.float32
BF16 = jnp.bfloat16

GRID_W = 64
DIFF_HEAD_DIM = 64
GQA_HEAD_DIM = 128
GQA_KV_HEADS = 2
ROPE_THETA = 10000.0
N_GROUPS = 4
EXPERTS_PER_GROUP = 8
N_EXPERTS = N_GROUPS * EXPERTS_PER_GROUP
N_MOD = 6
LN_EPS = 1e-5
RMS_EPS = 1e-6
DEPTH = 2
DEEPNORM_ALPHA = (2 * DEPTH) ** 0.25
LOG2E = 1.4426950408889634

LANES = 128
TM = 256
VMEM_LIMIT = 48 * 1024 * 1024

GROUP_LANE0 = N_EXPERTS


def _cparams(sem):
    return pltpu.CompilerParams(dimension_semantics=sem, vmem_limit_bytes=VMEM_LIMIT)


def _mod_kernel(c_ref, w_ref, b_ref, o_ref):
    c = c_ref[...]
    s = c * jax.nn.sigmoid(c)
    w = w_ref[...]
    sh = s.astype(BF16)
    sl = (s - sh.astype(F32)).astype(BF16)
    wh = w.astype(BF16)
    wl = (w - wh.astype(F32)).astype(BF16)
    acc = jnp.dot(sh, wh, preferred_element_type=F32)
    acc += jnp.dot(sl, wh, preferred_element_type=F32)
    acc += jnp.dot(sh, wl, preferred_element_type=F32)
    o_ref[...] = acc + b_ref[...]


def _modulation(cond, w_mod, b_mod):
    depth, d, width = w_mod.shape
    r = cond.shape[0]
    tn = 512
    return pl.pallas_call(
        _mod_kernel,
        grid=(depth, width // tn),
        in_specs=[
            pl.BlockSpec((r, d), lambda i, j: (0, 0)),
            pl.BlockSpec((None, d, tn), lambda i, j: (i, 0, j)),
            pl.BlockSpec((None, 1, tn), lambda i, j: (i, 0, j)),
        ],
        out_specs=pl.BlockSpec((None, r, tn), lambda i, j: (i, 0, j)),
        out_shape=jax.ShapeDtypeStruct((depth, r, width), F32),
        compiler_params=_cparams(("arbitrary", "arbitrary")),
        name="modulation",
    )(cond, w_mod, b_mod.reshape(depth, 1, width))


def _rope_tables(n_ctx, n, head_dim):
    rows = n // GRID_W
    row = jnp.broadcast_to(jnp.arange(rows, dtype=F32)[:, None], (rows, GRID_W)).reshape(-1)
    col = jnp.broadcast_to(jnp.arange(GRID_W, dtype=F32)[None, :], (rows, GRID_W)).reshape(-1)
    axis_dim = head_dim // 2
    inv_freq = ROPE_THETA ** (-jnp.arange(0, axis_dim, 2, dtype=F32) / axis_dim)
    ang = jnp.stack([row, col], axis=-1)[:, :, None] * inv_freq
    cos, sin = jnp.cos(ang), jnp.sin(ang)
    zero = jnp.zeros_like(sin)
    c = jnp.concatenate([cos, cos], axis=-1).reshape(n, head_dim)
    s_first = jnp.concatenate([-sin, zero], axis=-1).reshape(n, head_dim)
    s_second = jnp.concatenate([zero, sin], axis=-1).reshape(n, head_dim)
    tab = jnp.stack([c, s_first, s_second])
    tab = jnp.tile(tab, (1, 1, LANES // head_dim))
    ident = jnp.stack([jnp.ones((n_ctx, LANES), F32), jnp.zeros((n_ctx, LANES), F32),
                       jnp.zeros((n_ctx, LANES), F32)])
    return jnp.concatenate([ident, tab], axis=1)


def _rope_chunk(x, c, s_first, s_second, quarter):
    return x * c + pltpu.roll(x, LANES - quarter, 1) * s_first + pltpu.roll(x, quarter, 1) * s_second


def _qkv_diff_kernel(x_ref, mod_ref, w_ref, tab_ref, q_ref, k_ref, v_ref, *, d, qscale):
    x = x_ref[...]
    shift = mod_ref[:, 0:d]
    scale = mod_ref[:, d:2 * d]
    h = (x * (1.0 + scale) + shift).astype(BF16)
    qkv = jnp.dot(h, w_ref[...], preferred_element_type=F32)
    c, s_first, s_second = tab_ref[0], tab_ref[1], tab_ref[2]
    quarter = DIFF_HEAD_DIM // 4
    for j in range(d // LANES):
        lo, hi = j * LANES, (j + 1) * LANES
        q = _rope_chunk(qkv[:, lo:hi], c, s_first, s_second, quarter)
        q_ref[:, lo:hi] = (q * qscale).astype(BF16)
        k = _rope_chunk(qkv[:, d + lo:d + hi], c, s_first, s_second, quarter)
        k_ref[:, lo:hi] = k.astype(BF16)
    v_ref[...] = qkv[:, 2 * d:].astype(BF16)


def _rms_head(x, g):
    return x * lax.rsqrt(jnp.mean(x * x, axis=-1, keepdims=True) + RMS_EPS) * g


def _qkv_gqa_kernel(x_ref, mod_ref, w_ref, tab_ref, qg_ref, kg_ref, q_ref, k_ref, v_ref, *, d, qscale):
    x = x_ref[...]
    shift = mod_ref[:, 0:d]
    scale = mod_ref[:, d:2 * d]
    h = (x * (1.0 + scale) + shift).astype(BF16)
    qkv = jnp.dot(h, w_ref[...], preferred_element_type=F32)
    c, s_first, s_second = tab_ref[0], tab_ref[1], tab_ref[2]
    quarter = GQA_HEAD_DIM // 4
    kv_w = GQA_KV_HEADS * GQA_HEAD_DIM
    for j in range(d // LANES):
        lo, hi = j * LANES, (j + 1) * LANES
        q = _rope_chunk(_rms_head(qkv[:, lo:hi], qg_ref[...]), c, s_first, s_second, quarter)
        q_ref[:, lo:hi] = (q * qscale).astype(BF16)
    for j in range(GQA_KV_HEADS):
        lo, hi = j * LANES, (j + 1) * LANES
        k = _rope_chunk(_rms_head(qkv[:, d + lo:d + hi], kg_ref[...]), c, s_first, s_second, quarter)
        k_ref[:, lo:hi] = k.astype(BF16)
    v_ref[...] = qkv[:, d + kv_w:].astype(BF16)


def _mod_row_map(tiles_per_sample, ctx_tiles, ctx_row):
    def index(i):
        b = i // tiles_per_sample
        j = i % tiles_per_sample
        return (jnp.where(j < ctx_tiles, ctx_row, b), 0, 0)
    return index


def _qkv_proj(kind, xs, mod, w, tab, norm_g, tiles_per_sample, ctx_tiles, ctx_row):
    rows, d = xs.shape
    n_tiles = rows // TM
    width = w.shape[1]
    row_map = _mod_row_map(tiles_per_sample, ctx_tiles, ctx_row)
    in_specs = [
        pl.BlockSpec((TM, d), lambda i: (i, 0)),
        pl.BlockSpec((None, 1, mod.shape[-1]), row_map),
        pl.BlockSpec((d, width), lambda i: (0, 0)),
        pl.BlockSpec((3, TM, LANES), lambda i: (0, i % tiles_per_sample, 0)),
    ]
    args = [xs, mod, w, tab]
    if kind == "diff":
        kern = functools.partial(_qkv_diff_kernel, d=d, qscale=DIFF_HEAD_DIM ** -0.5 * LOG2E)
        kw, vw = d, d
    else:
        kern = functools.partial(_qkv_gqa_kernel, d=d, qscale=GQA_HEAD_DIM ** -0.5 * LOG2E)
        kw = vw = GQA_KV_HEADS * GQA_HEAD_DIM
        in_specs += [pl.BlockSpec((1, LANES), lambda i: (0, 0))] * 2
        args += list(norm_g)
    return pl.pallas_call(
        kern,
        grid=(n_tiles,),
        in_specs=in_specs,
        out_specs=[
            pl.BlockSpec((TM, d), lambda i: (i, 0)),
            pl.BlockSpec((TM, kw), lambda i: (i, 0)),
            pl.BlockSpec((TM, vw), lambda i: (i, 0)),
        ],
        out_shape=[
            jax.ShapeDtypeStruct((rows, d), BF16),
            jax.ShapeDtypeStruct((rows, kw), BF16),
            jax.ShapeDtypeStruct((rows, vw), BF16),
        ],
        compiler_params=_cparams(("arbitrary",)),
        name="qkv_" + kind,
    )(*args)


def _softmax_pv(q, k, v):
    s = lax.dot_general(q, k, (((1,), (1,)), ((), ())), preferred_element_type=F32)
    m = jnp.max(s, axis=-1, keepdims=True)
    p = jnp.exp2(s - m)
    l = jnp.sum(p, axis=-1, keepdims=True)
    o = jnp.dot(p.astype(BF16), v, preferred_element_type=F32)
    return o / l


def _diff_attn_kernel(lam_ref, g_ref, q_ref, k_ref, v_ref, o_ref, *, n_ctx, ctx_tiles, lambda_init):
    qi = pl.program_id(2)
    lv = lam_ref[...]
    lam = (jnp.exp(jnp.sum(lv[0:1] * lv[1:2], axis=-1, keepdims=True))
           - jnp.exp(jnp.sum(lv[2:3] * lv[3:4], axis=-1, keepdims=True)) + lambda_init)
    q = q_ref[...]
    lane = lax.broadcasted_iota(jnp.int32, q.shape, 1)
    zero = jnp.zeros_like(q)
    q1 = jnp.where(lane < DIFF_HEAD_DIM, q, zero)
    q2 = jnp.where(lane >= DIFF_HEAD_DIM, q, zero)

    def attend(k, v):
        o = _softmax_pv(q1, k, v) - lam * _softmax_pv(q2, k, v)
        o = o * lax.rsqrt(jnp.mean(o * o, axis=-1, keepdims=True) + RMS_EPS) * g_ref[...]
        o_ref[...] = (o * (1.0 - lambda_init)).astype(o_ref.dtype)

    @pl.when(qi < ctx_tiles)
    def _():
        attend(k_ref[0:n_ctx, :], v_ref[0:n_ctx, :])

    @pl.when(qi >= ctx_tiles)
    def _():
        attend(k_ref[...], v_ref[...])


def _diff_attention(q, k, v, lam_vecs, subln_g, batch, t, n_ctx, lambda_init):
    rows, d = q.shape
    heads = d // LANES
    tiles = t // TM
    kern = functools.partial(_diff_attn_kernel, n_ctx=n_ctx, ctx_tiles=n_ctx // TM, lambda_init=lambda_init)
    return pl.pallas_call(
        kern,
        grid=(batch, heads, tiles),
        in_specs=[
            pl.BlockSpec(lam_vecs.shape, lambda b, h, i: (0, 0)),
            pl.BlockSpec((1, LANES), lambda b, h, i: (0, 0)),
            pl.BlockSpec((TM, LANES), lambda b, h, i: (b * tiles + i, h)),
            pl.BlockSpec((t, LANES), lambda b, h, i: (b, h)),
            pl.BlockSpec((t, LANES), lambda b, h, i: (b, h)),
        ],
        out_specs=pl.BlockSpec((TM, LANES), lambda b, h, i: (b * tiles + i, h)),
        out_shape=jax.ShapeDtypeStruct((rows, d), BF16),
        compiler_params=_cparams(("arbitrary", "arbitrary", "arbitrary")),
        name="diff_attention",
    )(lam_vecs, subln_g, q, k, v)


def _gqa_attn_kernel(q_ref, k_ref, v_ref, o_ref, *, group):
    k = k_ref[...]
    v = v_ref[...]
    for g in range(group):
        lo, hi = g * LANES, (g + 1) * LANES
        o_ref[:, lo:hi] = _softmax_pv(q_ref[:, lo:hi], k, v).astype(o_ref.dtype)


def _gqa_attention(q, k, v, batch, t, n_ctx):
    rows, d = q.shape
    group = d // GQA_HEAD_DIM // GQA_KV_HEADS
    tiles = t // TM
    ctx_tiles = n_ctx // TM
    q_tiles = tiles - ctx_tiles
    gw = group * LANES
    return pl.pallas_call(
        functools.partial(_gqa_attn_kernel, group=group),
        grid=(batch, GQA_KV_HEADS, q_tiles),
        in_specs=[
            pl.BlockSpec((TM, gw), lambda b, h, i: (b * tiles + ctx_tiles + i, h)),
            pl.BlockSpec((t, LANES), lambda b, h, i: (b, h)),
            pl.BlockSpec((t, LANES), lambda b, h, i: (b, h)),
        ],
        out_specs=pl.BlockSpec((TM, gw), lambda b, h, i: (b * q_tiles + i, h)),
        out_shape=jax.ShapeDtypeStruct((batch * q_tiles * TM, d), BF16),
        compiler_params=_cparams(("arbitrary", "arbitrary", "arbitrary")),
        name="gqa_attention",
    )(q, k, v)


def _layer_norm(y, g, b):
    mu = jnp.mean(y, axis=-1, keepdims=True)
    yc = y - mu
    var = jnp.mean(yc * yc, axis=-1, keepdims=True)
    return yc * lax.rsqrt(var + LN_EPS) * g + b


def _proj_route_kernel(o_ref, x_ref, mod_ref, wo_ref, lng_ref, lnb_ref, wrh_ref, wrl_ref, br_ref,
                       x1_ref, h2_ref, route_ref, cnt_ref, carry_ref, *, d):
    step = pl.program_id(0)

    @pl.when(step == 0)
    def _():
        carry_ref[...] = jnp.zeros_like(carry_ref)

    gate_m = mod_ref[:, 2 * d:3 * d]
    shift_f = mod_ref[:, 3 * d:4 * d]
    scale_f = mod_ref[:, 4 * d:5 * d]
    ox = jnp.dot(o_ref[...], wo_ref[...], preferred_element_type=F32)
    x1 = _layer_norm(DEEPNORM_ALPHA * x_ref[...] + gate_m * ox, lng_ref[...], lnb_ref[...])
    x1_ref[...] = x1
    h2 = x1 * (1.0 + scale_f) + shift_f
    h2_ref[...] = h2

    hh = h2.astype(BF16)
    hl = (h2 - hh.astype(F32)).astype(BF16)
    logits = jnp.dot(hh, wrh_ref[...], preferred_element_type=F32)
    logits += jnp.dot(hl, wrh_ref[...], preferred_element_type=F32)
    logits += jnp.dot(hh, wrl_ref[...], preferred_element_type=F32)
    logits += br_ref[...]

    lane = lax.broadcasted_iota(jnp.int32, logits.shape, 1).astype(F32)
    neg = jnp.full_like(logits, -jnp.inf)
    big = jnp.full_like(logits, 1e9)
    is_group = (lane >= GROUP_LANE0) & (lane < GROUP_LANE0 + N_GROUPS)
    lg = jnp.where(is_group, logits, neg)
    g_max = jnp.max(lg, axis=-1, keepdims=True)
    g_idx = jnp.min(jnp.where(lg == g_max, lane - GROUP_LANE0, big), axis=-1, keepdims=True)
    g_top = 1.0 / jnp.sum(jnp.exp(lg - g_max), axis=-1, keepdims=True)

    lane_group = jnp.floor(lane * (1.0 / EXPERTS_PER_GROUP))
    in_group = (lane < N_EXPERTS) & (lane_group == g_idx)
    le = jnp.where(in_group, logits, neg)
    m1 = jnp.max(le, axis=-1, keepdims=True)
    i1 = jnp.min(jnp.where(le == m1, lane, big), axis=-1, keepdims=True)
    le2 = jnp.where(lane == i1, neg, le)
    m2 = jnp.max(le2, axis=-1, keepdims=True)
    i2 = jnp.min(jnp.where(le2 == m2, lane, big), axis=-1, keepdims=True)
    r = jnp.exp(m2 - m1)
    w1 = g_top / (1.0 + r)
    w2 = g_top * r / (1.0 + r)

    a1 = (lane == i1).astype(F32)
    a2 = (lane == i2).astype(F32)
    both = (a1 + a2).astype(BF16)
    tm = logits.shape[0]
    rr = lax.broadcasted_iota(jnp.int32, (tm, tm), 0)
    cc = lax.broadcasted_iota(jnp.int32, (tm, tm), 1)
    strict_lower = (rr > cc).astype(BF16)
    before = jnp.dot(strict_lower, both, preferred_element_type=F32) + carry_ref[0:1, :]
    rank1 = jnp.sum(a1 * before, axis=-1, keepdims=True)
    rank2 = jnp.sum(a2 * before, axis=-1, keepdims=True)
    carry_ref[0:1, :] = carry_ref[0:1, :] + jnp.sum(a1 + a2, axis=0, keepdims=True)
    cnt_ref[...] = carry_ref[...]

    out = jnp.zeros_like(logits)
    for idx, val in enumerate((i1, i2, w1, w2, rank1, rank2)):
        out = jnp.where(lane == float(idx), val, out)
    route_ref[...] = out


def _proj_route(o, xs, mod, wo, ln_g, ln_b, wr_hi, wr_lo, br, tiles_per_sample, ctx_tiles, ctx_row,
                skip_ctx):
    rows, d = xs.shape
    n_tiles = o.shape[0] // TM
    q_tiles = tiles_per_sample - ctx_tiles
    if skip_ctx:
        tile_of = lambda i: (i // q_tiles) * tiles_per_sample + ctx_tiles + i % q_tiles
    else:
        tile_of = lambda i: i
    row_map = _mod_row_map(tiles_per_sample, ctx_tiles, ctx_row)
    const = lambda i: (0, 0)
    return pl.pallas_call(
        functools.partial(_proj_route_kernel, d=d),
        grid=(n_tiles,),
        in_specs=[
            pl.BlockSpec((TM, d), lambda i: (i, 0)),
            pl.BlockSpec((TM, d), lambda i: (tile_of(i), 0)),
            pl.BlockSpec((None, 1, mod.shape[-1]), lambda i: row_map(tile_of(i))),
            pl.BlockSpec((d, d), const),
            pl.BlockSpec((1, d), const),
            pl.BlockSpec((1, d), const),
            pl.BlockSpec((d, LANES), const),
            pl.BlockSpec((d, LANES), const),
            pl.BlockSpec((1, LANES), const),
        ],
        out_specs=[
            pl.BlockSpec((TM, d), lambda i: (i, 0)),
            pl.BlockSpec((TM, d), lambda i: (i, 0)),
            pl.BlockSpec((TM, LANES), lambda i: (i, 0)),
            pl.BlockSpec((8, LANES), const),
        ],
        out_shape=[
            jax.ShapeDtypeStruct((n_tiles * TM, d), F32),
            jax.ShapeDtypeStruct((n_tiles * TM, d), F32),
            jax.ShapeDtypeStruct((n_tiles * TM, LANES), F32),
            jax.ShapeDtypeStruct((8, LANES), F32),
        ],
        scratch_shapes=[pltpu.VMEM((8, LANES), F32)],
        compiler_params=_cparams(("arbitrary",)),
        name="proj_route",
    )(o, xs, mod, wo, ln_g, ln_b, wr_hi, wr_lo, br)


def _row_gather(src_hbm, idx_ref, base, dst, sem, rows):
    def body(r, carry):
        tok = idx_ref[base + r]
        pltpu.make_async_copy(src_hbm.at[pl.ds(tok, 1), :], dst.at[pl.ds(r, 1), :], sem).start()
        return carry
    lax.fori_loop(0, rows, body, 0, unroll=8)


def _wait_rows(src_hbm, dst, sem, rows):
    pltpu.make_async_copy(src_hbm.at[pl.ds(0, rows), :], dst, sem).wait()


def _expert_kernel(te_ref, src_ref, nt_ref, h_hbm, wg_ref, wu_ref, wd_ref, y_ref,
                   xbuf, sems, wgu_b, wd_b, *, hidden):
    t = pl.program_id(0)
    nt = nt_ref[0]
    slot = t % 2

    @pl.when(t == 0)
    def _():
        _row_gather(h_hbm, src_ref, 0, xbuf.at[0], sems.at[0], TM)

    @pl.when(t + 1 < nt)
    def _():
        _row_gather(h_hbm, src_ref, (t + 1) * TM, xbuf.at[1 - slot], sems.at[1 - slot], TM)

    @pl.when(t < nt)
    def _():
        prev = te_ref[jnp.maximum(t - 1, 0)]

        @pl.when((t == 0) | (te_ref[t] != prev))
        def _():
            wgu_b[:, 0:hidden] = wg_ref[...].astype(BF16)
            wgu_b[:, hidden:2 * hidden] = wu_ref[...].astype(BF16)
            wd_b[...] = wd_ref[...].astype(BF16)

        _wait_rows(h_hbm, xbuf.at[slot], sems.at[slot], TM)
        x = xbuf[slot].astype(BF16)
        au = jnp.dot(x, wgu_b[...], preferred_element_type=F32)
        a = au[:, 0:hidden]
        u = au[:, hidden:2 * hidden]
        act = (a * jax.nn.sigmoid(a) * u).astype(BF16)
        y_ref[...] = jnp.dot(act, wd_b[...], preferred_element_type=F32)

    @pl.when(t >= nt)
    def _():
        y_ref[...] = jnp.zeros_like(y_ref)


def _experts(h2, tile_expert, src, n_tiles, w_gate, w_up, w_down, max_tiles):
    d = h2.shape[1]
    hidden = w_gate.shape[-1]
    last = lambda t, te, s, nt: jnp.minimum(t, nt[0] - 1)
    return pl.pallas_call(
        functools.partial(_expert_kernel, hidden=hidden),
        grid_spec=pltpu.PrefetchScalarGridSpec(
            num_scalar_prefetch=3,
            grid=(max_tiles,),
            in_specs=[
                pl.BlockSpec(memory_space=pl.ANY),
                pl.BlockSpec((None, d, hidden), lambda t, te, s, nt: (te[last(t, te, s, nt)], 0, 0)),
                pl.BlockSpec((None, d, hidden), lambda t, te, s, nt: (te[last(t, te, s, nt)], 0, 0)),
                pl.BlockSpec((None, hidden, d), lambda t, te, s, nt: (te[last(t, te, s, nt)], 0, 0)),
            ],
            out_specs=pl.BlockSpec((TM, d), lambda t, te, s, nt: (t, 0)),
            scratch_shapes=[
                pltpu.VMEM((2, TM, d), F32),
                pltpu.SemaphoreType.DMA((2,)),
                pltpu.VMEM((d, 2 * hidden), BF16),
                pltpu.VMEM((hidden, d), BF16),
            ],
        ),
        out_shape=jax.ShapeDtypeStruct((max_tiles * TM, d), F32),
        compiler_params=_cparams(("arbitrary",)),
        name="experts",
    )(tile_expert, src, n_tiles, h2, w_gate, w_up, w_down)


def _combine_kernel(pos1_ref, pos2_ref, y_hbm, x1_ref, route_ref, mod_ref, lng_ref, lnb_ref, o_ref,
                    ybuf, sems, *, d, n_tiles):
    t = pl.program_id(0)
    slot = t % 2

    def issue(tile, s):
        _row_gather(y_hbm, pos1_ref, tile * TM, ybuf.at[s, 0], sems.at[s], TM)
        _row_gather(y_hbm, pos2_ref, tile * TM, ybuf.at[s, 1], sems.at[s], TM)

    @pl.when(t == 0)
    def _():
        issue(0, 0)

    @pl.when(t + 1 < n_tiles)
    def _():
        issue(t + 1, 1 - slot)

    _wait_rows(y_hbm, ybuf.at[slot, 0], sems.at[slot], TM)
    _wait_rows(y_hbm, ybuf.at[slot, 1], sems.at[slot], TM)
    route = route_ref[...]
    w1 = route[:, 2:3]
    w2 = route[:, 3:4]
    fx = w1 * ybuf[slot, 0] + w2 * ybuf[slot, 1]
    gate_f = mod_ref[:, 5 * d:6 * d]
    o_ref[...] = _layer_norm(DEEPNORM_ALPHA * x1_ref[...] + gate_f * fx, lng_ref[...], lnb_ref[...])


def _combine(ys, pos1, pos2, x1, route, mod, ln_g, ln_b, mod_tile_of):
    rows, d = x1.shape
    n_tiles = rows // TM
    const = lambda t, p1, p2: (0, 0)
    return pl.pallas_call(
        functools.partial(_combine_kernel, d=d, n_tiles=n_tiles),
        grid_spec=pltpu.PrefetchScalarGridSpec(
            num_scalar_prefetch=2,
            grid=(n_tiles,),
            in_specs=[
                pl.BlockSpec(memory_space=pl.ANY),
                pl.BlockSpec((TM, d), lambda t, p1, p2: (t, 0)),
                pl.BlockSpec((TM, LANES), lambda t, p1, p2: (t, 0)),
                pl.BlockSpec((None, 1, mod.shape[-1]), lambda t, p1, p2: mod_tile_of(t)),
                pl.BlockSpec((1, d), const),
                pl.BlockSpec((1, d), const),
            ],
            out_specs=pl.BlockSpec((TM, d), lambda t, p1, p2: (t, 0)),
            scratch_shapes=[
                pltpu.VMEM((2, 2, TM, d), F32),
                pltpu.SemaphoreType.DMA((2,)),
            ],
        ),
        out_shape=jax.ShapeDtypeStruct((rows, d), F32),
        compiler_params=_cparams(("arbitrary",)),
        name="combine",
    )(pos1, pos2, ys, x1, route, mod, ln_g, ln_b)


def _dispatch_plan(route, counts, max_tiles):
    n_tok = route.shape[0]
    e = route[:, 0:2].astype(jnp.int32)
    rank = route[:, 4:6].astype(jnp.int32)
    cnt = counts[0, :N_EXPERTS].astype(jnp.int32)
    tiles_e = (cnt + TM - 1) // TM
    tile_end = jnp.cumsum(tiles_e)
    offs = (tile_end - tiles_e) * TM
    pos = offs[e] + rank
    n_tiles = tile_end[-1]
    tile_expert = jnp.searchsorted(tile_end, jnp.arange(max_tiles, dtype=jnp.int32), side="right")
    tile_expert = jnp.minimum(tile_expert, N_EXPERTS - 1).astype(jnp.int32)
    tok = jnp.arange(n_tok, dtype=jnp.int32)
    src = jnp.zeros((max_tiles * TM,), jnp.int32)
    src = src.at[pos[:, 0]].set(tok).at[pos[:, 1]].set(tok)
    return pos[:, 0], pos[:, 1], src, tile_expert, n_tiles.reshape(1).astype(jnp.int32)


def kernel(x, c, ctx, c_ctx, w_mod, b_mod, ln_mix_g, ln_mix_b, ln_ffn_g, ln_ffn_b, diff_w_qkv, diff_w_o, diff_lambda_q1, diff_lambda_k1, diff_lambda_q2, diff_lambda_k2, diff_subln_g, gqa_w_qkv, gqa_w_o, gqa_q_norm_g, gqa_k_norm_g, moe_w_group, moe_b_group, moe_w_router, moe_b_router, moe_w_gate, moe_w_up, moe_w_down):
    batch, n, d = x.shape
    n_ctx = ctx.shape[1]
    t = n_ctx + n
    assert n % TM == 0 and n_ctx % TM == 0 and n % GRID_W == 0 and d % LANES == 0
    assert w_mod.shape[0] == DEPTH
    tiles_per_sample = t // TM
    ctx_tiles = n_ctx // TM
    q_tiles = tiles_per_sample - ctx_tiles

    pad = (-(batch + 1)) % 8
    cond = jnp.concatenate([c, c_ctx[None, :], jnp.zeros((pad, d), F32)], axis=0)
    ctx_row = batch
    mod_all = _modulation(cond, w_mod, b_mod)
    mod_all = mod_all.reshape(DEPTH, cond.shape[0], 1, N_MOD * d)

    xs = jnp.concatenate([ctx, x], axis=1).reshape(batch * t, d)

    for i in range(DEPTH):
        last = i == DEPTH - 1
        mod = mod_all[i]
        j = i // 2
        lng_m, lnb_m = ln_mix_g[i][None, :], ln_mix_b[i][None, :]
        lng_f, lnb_f = ln_ffn_g[i][None, :], ln_ffn_b[i][None, :]
        if i % 2 == 0:
            lambda_init = 0.8 - 0.6 * math.exp(-0.3 * i)
            tab = _rope_tables(n_ctx, n, DIFF_HEAD_DIM)
            q, k, v = _qkv_proj("diff", xs, mod, diff_w_qkv[j].astype(BF16), tab, None,
                                tiles_per_sample, ctx_tiles, ctx_row)
            lam_vecs = jnp.stack([diff_lambda_q1[j], diff_lambda_k1[j], diff_lambda_q2[j], diff_lambda_k2[j]])
            o = _diff_attention(q, k, v, lam_vecs.astype(F32), diff_subln_g[j][None, :], batch, t, n_ctx,
                                lambda_init)
            if last:
                o = o.reshape(batch, t, d)[:, n_ctx:].reshape(batch * n, d)
            wo = diff_w_o[j]
        else:
            tab = _rope_tables(n_ctx, n, GQA_HEAD_DIM)
            q, k, v = _qkv_proj("gqa", xs, mod, gqa_w_qkv[j].astype(BF16), tab,
                                (gqa_q_norm_g[j][None, :], gqa_k_norm_g[j][None, :]),
                                tiles_per_sample, ctx_tiles, ctx_row)
            o = _gqa_attention(q, k, v, batch, t, n_ctx)
            if not last:
                raise NotImplementedError("grouped-query layer with context outputs")
            wo = gqa_w_o[j]

        w_r = jnp.zeros((d, LANES), F32)
        w_r = w_r.at[:, :N_EXPERTS].set(moe_w_router[i]).at[:, GROUP_LANE0:GROUP_LANE0 + N_GROUPS].set(moe_w_group[i])
        b_r = jnp.zeros((1, LANES), F32)
        b_r = b_r.at[0, :N_EXPERTS].set(moe_b_router[i]).at[0, GROUP_LANE0:GROUP_LANE0 + N_GROUPS].set(moe_b_group[i])
        wr_hi = w_r.astype(BF16)
        wr_lo = (w_r - wr_hi.astype(F32)).astype(BF16)

        x1, h2, route, counts = _proj_route(o, xs, mod, wo.astype(BF16), lng_m, lnb_m, wr_hi, wr_lo, b_r,
                                            tiles_per_sample, ctx_tiles, ctx_row, skip_ctx=last)
        n_tok = x1.shape[0]
        max_tiles = (2 * n_tok) // TM + N_EXPERTS
        pos1, pos2, src, tile_expert, n_tiles = _dispatch_plan(route, counts, max_tiles)
        w_gate = moe_w_gate[i].reshape(N_EXPERTS, d, -1)
        w_up = moe_w_up[i].reshape(N_EXPERTS, d, -1)
        w_down = moe_w_down[i].reshape(N_EXPERTS, -1, d)
        ys = _experts(h2, tile_expert, src, n_tiles, w_gate, w_up, w_down, max_tiles)

        row_map = _mod_row_map(tiles_per_sample, ctx_tiles, ctx_row)
        if last:
            mod_tile_of = lambda tt: row_map((tt // q_tiles) * tiles_per_sample + ctx_tiles + tt % q_tiles)
        else:
            mod_tile_of = row_map
        xs = _combine(ys, pos1, pos2, x1, route, mod, lng_f, lnb_f, mod_tile_of)

    return xs.reshape(batch, n, d)
```

```python
import functools
import math

import jax
import jax.numpy as jnp
from jax import lax
from jax.experimental import pallas as pl
from jax.experimental.pallas import tpu as pltpu

F32 = jnp.float32
BF16 = jnp.bfloat16

GRID_W = 64
DIFF_HEAD_DIM = 64
GQA_HEAD_DIM = 128
GQA_KV_HEADS = 2
ROPE_THETA = 10000.0
N_GROUPS = 4
EXPERTS_PER_GROUP = 8
N_EXPERTS = N_GROUPS * EXPERTS_PER_GROUP
N_MOD = 6
LN_EPS = 1e-5
RMS_EPS = 1e-6
DEPTH = 2
DEEPNORM_ALPHA = (2 * DEPTH) ** 0.25
LOG2E = 1.4426950408889634

LANES = 128
SUBLANES = 8
TM = 256
VMEM_LIMIT = 48 * 1024 * 1024

GROUP_LANE0 = N_EXPERTS

CHUNK = SUBLANES
UNITS_PER_TILE = TM // CHUNK
CHUNK_BITS = UNITS_PER_TILE.bit_length()
LOCAL_ROWS = 2 * TM + N_EXPERTS * CHUNK
ROUTE_E, ROUTE_W, ROUTE_RANK = 0, 2, 4


def _cparams(sem):
    return pltpu.CompilerParams(dimension_semantics=sem, vmem_limit_bytes=VMEM_LIMIT)


def _mod_kernel(c_ref, w_ref, b_ref, o_ref):
    c = c_ref[...]
    s = c * jax.nn.sigmoid(c)
    w = w_ref[...]
    sh = s.astype(BF16)
    sl = (s - sh.astype(F32)).astype(BF16)
    wh = w.astype(BF16)
    wl = (w - wh.astype(F32)).astype(BF16)
    acc = jnp.dot(sh, wh, preferred_element_type=F32)
    acc += jnp.dot(sl, wh, preferred_element_type=F32)
    acc += jnp.dot(sh, wl, preferred_element_type=F32)
    o_ref[...] = acc + b_ref[...]


def _modulation(cond, w_mod, b_mod):
    depth, d, width = w_mod.shape
    r = cond.shape[0]
    tn = 512
    return pl.pallas_call(
        _mod_kernel,
        grid=(depth, width // tn),
        in_specs=[
            pl.BlockSpec((r, d), lambda i, j: (0, 0)),
            pl.BlockSpec((None, d, tn), lambda i, j: (i, 0, j)),
            pl.BlockSpec((None, 1, tn), lambda i, j: (i, 0, j)),
        ],
        out_specs=pl.BlockSpec((None, r, tn), lambda i, j: (i, 0, j)),
        out_shape=jax.ShapeDtypeStruct((depth, r, width), F32),
        compiler_params=_cparams(("arbitrary", "arbitrary")),
        name="modulation",
    )(cond, w_mod, b_mod.reshape(depth, 1, width))


def _rope_tables(n_ctx, n, head_dim):
    rows = n // GRID_W
    row = jnp.broadcast_to(jnp.arange(rows, dtype=F32)[:, None], (rows, GRID_W)).reshape(-1)
    col = jnp.broadcast_to(jnp.arange(GRID_W, dtype=F32)[None, :], (rows, GRID_W)).reshape(-1)
    axis_dim = head_dim // 2
    inv_freq = ROPE_THETA ** (-jnp.arange(0, axis_dim, 2, dtype=F32) / axis_dim)
    ang = jnp.stack([row, col], axis=-1)[:, :, None] * inv_freq
    cos, sin = jnp.cos(ang), jnp.sin(ang)
    zero = jnp.zeros_like(sin)
    c = jnp.concatenate([cos, cos], axis=-1).reshape(n, head_dim)
    s_first = jnp.concatenate([-sin, zero], axis=-1).reshape(n, head_dim)
    s_second = jnp.concatenate([zero, sin], axis=-1).reshape(n, head_dim)
    tab = jnp.stack([c, s_first, s_second])
    tab = jnp.tile(tab, (1, 1, LANES // head_dim))
    ident = jnp.stack([jnp.ones((n_ctx, LANES), F32), jnp.zeros((n_ctx, LANES), F32),
                       jnp.zeros((n_ctx, LANES), F32)])
    return jnp.concatenate([ident, tab], axis=1)


def _rope_chunk(x, c, s_first, s_second, quarter):
    return x * c + pltpu.roll(x, LANES - quarter, 1) * s_first + pltpu.roll(x, quarter, 1) * s_second


def _qkv_diff_kernel(x_ref, mod_ref, w_ref, tab_ref, q_ref, k_ref, v_ref, *, d, qscale):
    x = x_ref[...]
    shift = mod_ref[:, 0:d]
    scale = mod_ref[:, d:2 * d]
    h = (x * (1.0 + scale) + shift).astype(BF16)
    qkv = jnp.dot(h, w_ref[...], preferred_element_type=F32)
    c, s_first, s_second = tab_ref[0], tab_ref[1], tab_ref[2]
    quarter = DIFF_HEAD_DIM // 4
    for j in range(d // LANES):
        lo, hi = j * LANES, (j + 1) * LANES
        q = _rope_chunk(qkv[:, lo:hi], c, s_first, s_second, quarter)
        q_ref[:, lo:hi] = (q * qscale).astype(BF16)
        k = _rope_chunk(qkv[:, d + lo:d + hi], c, s_first, s_second, quarter)
        k_ref[:, lo:hi] = k.astype(BF16)
    v_ref[...] = qkv[:, 2 * d:].astype(BF16)


def _rms_head(x, g):
    return x * lax.rsqrt(jnp.mean(x * x, axis=-1, keepdims=True) + RMS_EPS) * g


def _qkv_gqa_kernel(x_ref, mod_ref, w_ref, tab_ref, qg_ref, kg_ref, q_ref, k_ref, v_ref, *, d, qscale):
    x = x_ref[...]
    shift = mod_ref[:, 0:d]
    scale = mod_ref[:, d:2 * d]
    h = (x * (1.0 + scale) + shift).astype(BF16)
    qkv = jnp.dot(h, w_ref[...], preferred_element_type=F32)
    c, s_first, s_second = tab_ref[0], tab_ref[1], tab_ref[2]
    quarter = GQA_HEAD_DIM // 4
    kv_w = GQA_KV_HEADS * GQA_HEAD_DIM
    for j in range(d // LANES):
        lo, hi = j * LANES, (j + 1) * LANES
        q = _rope_chunk(_rms_head(qkv[:, lo:hi], qg_ref[...]), c, s_first, s_second, quarter)
        q_ref[:, lo:hi] = (q * qscale).astype(BF16)
    for j in range(GQA_KV_HEADS):
        lo, hi = j * LANES, (j + 1) * LANES
        k = _rope_chunk(_rms_head(qkv[:, d + lo:d + hi], kg_ref[...]), c, s_first, s_second, quarter)
        k_ref[:, lo:hi] = k.astype(BF16)
    v_ref[...] = qkv[:, d + kv_w:].astype(BF16)


def _mod_row_map(tiles_per_sample, ctx_tiles, ctx_row):
    def index(i):
        b = i // tiles_per_sample
        j = i % tiles_per_sample
        return (jnp.where(j < ctx_tiles, ctx_row, b), 0, 0)
    return index


def _qkv_proj(kind, xs, mod, w, tab, norm_g, tiles_per_sample, ctx_tiles, ctx_row):
    rows, d = xs.shape
    n_tiles = rows // TM
    width = w.shape[1]
    row_map = _mod_row_map(tiles_per_sample, ctx_tiles, ctx_row)
    in_specs = [
        pl.BlockSpec((TM, d), lambda i: (i, 0)),
        pl.BlockSpec((None, 1, mod.shape[-1]), row_map),
        pl.BlockSpec((d, width), lambda i: (0, 0)),
        pl.BlockSpec((3, TM, LANES), lambda i: (0, i % tiles_per_sample, 0)),
    ]
    args = [xs, mod, w, tab]
    if kind == "diff":
        kern = functools.partial(_qkv_diff_kernel, d=d, qscale=DIFF_HEAD_DIM ** -0.5 * LOG2E)
        kw, vw = d, d
    else:
        kern = functools.partial(_qkv_gqa_kernel, d=d, qscale=GQA_HEAD_DIM ** -0.5 * LOG2E)
        kw = vw = GQA_KV_HEADS * GQA_HEAD_DIM
        in_specs += [pl.BlockSpec((1, LANES), lambda i: (0, 0))] * 2
        args += list(norm_g)
    return pl.pallas_call(
        kern,
        grid=(n_tiles,),
        in_specs=in_specs,
        out_specs=[
            pl.BlockSpec((TM, d), lambda i: (i, 0)),
            pl.BlockSpec((TM, kw), lambda i: (i, 0)),
            pl.BlockSpec((TM, vw), lambda i: (i, 0)),
        ],
        out_shape=[
            jax.ShapeDtypeStruct((rows, d), BF16),
            jax.ShapeDtypeStruct((rows, kw), BF16),
            jax.ShapeDtypeStruct((rows, vw), BF16),
        ],
        compiler_params=_cparams(("arbitrary",)),
        name="qkv_" + kind,
    )(*args)


def _softmax_pv(q, k, v):
    s = lax.dot_general(q, k, (((1,), (1,)), ((), ())), preferred_element_type=F32)
    m = jnp.max(s, axis=-1, keepdims=True)
    p = jnp.exp2(s - m)
    l = jnp.sum(p, axis=-1, keepdims=True)
    o = jnp.dot(p.astype(BF16), v, preferred_element_type=F32)
    return o / l


def _diff_attn_kernel(lam_ref, g_ref, q_ref, k_ref, v_ref, o_ref, *, n_ctx, ctx_tiles, lambda_init):
    qi = pl.program_id(2)
    lv = lam_ref[...]
    lam = (jnp.exp(jnp.sum(lv[0:1] * lv[1:2], axis=-1, keepdims=True))
           - jnp.exp(jnp.sum(lv[2:3] * lv[3:4], axis=-1, keepdims=True)) + lambda_init)
    q = q_ref[...]
    lane = lax.broadcasted_iota(jnp.int32, q.shape, 1)
    zero = jnp.zeros_like(q)
    q1 = jnp.where(lane < DIFF_HEAD_DIM, q, zero)
    q2 = jnp.where(lane >= DIFF_HEAD_DIM, q, zero)

    def attend(k, v):
        o = _softmax_pv(q1, k, v) - lam * _softmax_pv(q2, k, v)
        o = o * lax.rsqrt(jnp.mean(o * o, axis=-1, keepdims=True) + RMS_EPS) * g_ref[...]
        o_ref[...] = (o * (1.0 - lambda_init)).astype(o_ref.dtype)

    @pl.when(qi < ctx_tiles)
    def _():
        attend(k_ref[0:n_ctx, :], v_ref[0:n_ctx, :])

    @pl.when(qi >= ctx_tiles)
    def _():
        attend(k_ref[...], v_ref[...])


def _diff_attention(q, k, v, lam_vecs, subln_g, batch, t, n_ctx, lambda_init):
    rows, d = q.shape
    heads = d // LANES
    tiles = t // TM
    kern = functools.partial(_diff_attn_kernel, n_ctx=n_ctx, ctx_tiles=n_ctx // TM, lambda_init=lambda_init)
    return pl.pallas_call(
        kern,
        grid=(batch, heads, tiles),
        in_specs=[
            pl.BlockSpec(lam_vecs.shape, lambda b, h, i: (0, 0)),
            pl.BlockSpec((1, LANES), lambda b, h, i: (0, 0)),
            pl.BlockSpec((TM, LANES), lambda b, h, i: (b * tiles + i, h)),
            pl.BlockSpec((t, LANES), lambda b, h, i: (b, h)),
            pl.BlockSpec((t, LANES), lambda b, h, i: (b, h)),
        ],
        out_specs=pl.BlockSpec((TM, LANES), lambda b, h, i: (b * tiles + i, h)),
        out_shape=jax.ShapeDtypeStruct((rows, d), BF16),
        compiler_params=_cparams(("arbitrary", "arbitrary", "arbitrary")),
        name="diff_attention",
    )(lam_vecs, subln_g, q, k, v)


def _gqa_attn_kernel(q_ref, k_ref, v_ref, o_ref, *, group):
    k = k_ref[...]
    v = v_ref[...]
    for g in range(group):
        lo, hi = g * LANES, (g + 1) * LANES
        o_ref[:, lo:hi] = _softmax_pv(q_ref[:, lo:hi], k, v).astype(o_ref.dtype)


def _gqa_attention(q, k, v, batch, t, n_ctx):
    rows, d = q.shape
    group = d // GQA_HEAD_DIM // GQA_KV_HEADS
    tiles = t // TM
    ctx_tiles = n_ctx // TM
    q_tiles = tiles - ctx_tiles
    gw = group * LANES
    return pl.pallas_call(
        functools.partial(_gqa_attn_kernel, group=group),
        grid=(batch, GQA_KV_HEADS, q_tiles),
        in_specs=[
            pl.BlockSpec((TM, gw), lambda b, h, i: (b * tiles + ctx_tiles + i, h)),
            pl.BlockSpec((t, LANES), lambda b, h, i: (b, h)),
            pl.BlockSpec((t, LANES), lambda b, h, i: (b, h)),
        ],
        out_specs=pl.BlockSpec((TM, gw), lambda b, h, i: (b * q_tiles + i, h)),
        out_shape=jax.ShapeDtypeStruct((batch * q_tiles * TM, d), BF16),
        compiler_params=_cparams(("arbitrary", "arbitrary", "arbitrary")),
        name="gqa_attention",
    )(q, k, v)


def _layer_norm(y, g, b):
    mu = jnp.mean(y, axis=-1, keepdims=True)
    yc = y - mu
    var = jnp.mean(yc * yc, axis=-1, keepdims=True)
    return yc * lax.rsqrt(var + LN_EPS) * g + b


def _proj_route_kernel(o_ref, x_ref, mod_ref, wo_ref, lng_ref, lnb_ref, wrh_ref, wrl_ref, br_ref,
                       x1_ref, h2_ref, route_ref, cnt_ref, *, d):
    gate_m = mod_ref[:, 2 * d:3 * d]
    shift_f = mod_ref[:, 3 * d:4 * d]
    scale_f = mod_ref[:, 4 * d:5 * d]
    ox = jnp.dot(o_ref[...], wo_ref[...], preferred_element_type=F32)
    x1 = _layer_norm(DEEPNORM_ALPHA * x_ref[...] + gate_m * ox, lng_ref[...], lnb_ref[...])
    x1_ref[...] = x1
    h2 = x1 * (1.0 + scale_f) + shift_f
    h2_ref[...] = h2

    hh = h2.astype(BF16)
    hl = (h2 - hh.astype(F32)).astype(BF16)
    logits = jnp.dot(hh, wrh_ref[...], preferred_element_type=F32)
    logits += jnp.dot(hl, wrh_ref[...], preferred_element_type=F32)
    logits += jnp.dot(hh, wrl_ref[...], preferred_element_type=F32)
    logits += br_ref[...]

    lane = lax.broadcasted_iota(jnp.int32, logits.shape, 1).astype(F32)
    neg = jnp.full_like(logits, -jnp.inf)
    big = jnp.full_like(logits, 1e9)
    is_group = (lane >= GROUP_LANE0) & (lane < GROUP_LANE0 + N_GROUPS)
    lg = jnp.where(is_group, logits, neg)
    g_max = jnp.max(lg, axis=-1, keepdims=True)
    g_idx = jnp.min(jnp.where(lg == g_max, lane - GROUP_LANE0, big), axis=-1, keepdims=True)
    g_top = 1.0 / jnp.sum(jnp.exp(lg - g_max), axis=-1, keepdims=True)

    lane_group = jnp.floor(lane * (1.0 / EXPERTS_PER_GROUP))
    in_group = (lane < N_EXPERTS) & (lane_group == g_idx)
    le = jnp.where(in_group, logits, neg)
    m1 = jnp.max(le, axis=-1, keepdims=True)
    i1 = jnp.min(jnp.where(le == m1, lane, big), axis=-1, keepdims=True)
    le2 = jnp.where(lane == i1, neg, le)
    m2 = jnp.max(le2, axis=-1, keepdims=True)
    i2 = jnp.min(jnp.where(le2 == m2, lane, big), axis=-1, keepdims=True)
    r = jnp.exp(m2 - m1)
    w1 = g_top / (1.0 + r)
    w2 = g_top * r / (1.0 + r)

    a1 = (lane == i1).astype(F32)
    a2 = (lane == i2).astype(F32)
    both = (a1 + a2).astype(BF16)
    tm = logits.shape[0]
    rr = lax.broadcasted_iota(jnp.int32, (tm, tm), 0)
    cc = lax.broadcasted_iota(jnp.int32, (tm, tm), 1)
    strict_lower = (rr > cc).astype(BF16)
    before = jnp.dot(strict_lower, both, preferred_element_type=F32)
    rank1 = jnp.sum(a1 * before, axis=-1, keepdims=True)
    rank2 = jnp.sum(a2 * before, axis=-1, keepdims=True)
    cnt_ref[...] = jnp.sum(a1 + a2, axis=0, keepdims=True)

    out = jnp.zeros_like(logits)
    for idx, val in ((ROUTE_E, i1), (ROUTE_E + 1, i2), (ROUTE_W, w1), (ROUTE_W + 1, w2),
                     (ROUTE_RANK, rank1), (ROUTE_RANK + 1, rank2)):
        out = jnp.where(lane == float(idx), val, out)
    route_ref[...] = out


def _proj_route(o, xs, mod, wo, ln_g, ln_b, wr_hi, wr_lo, br, tiles_per_sample, ctx_tiles, ctx_row,
                skip_ctx):
    rows, d = xs.shape
    n_tiles = o.shape[0] // TM
    q_tiles = tiles_per_sample - ctx_tiles
    if skip_ctx:
        tile_of = lambda i: (i // q_tiles) * tiles_per_sample + ctx_tiles + i % q_tiles
    else:
        tile_of = lambda i: i
    row_map = _mod_row_map(tiles_per_sample, ctx_tiles, ctx_row)
    const = lambda i: (0, 0)
    return pl.pallas_call(
        functools.partial(_proj_route_kernel, d=d),
        grid=(n_tiles,),
        in_specs=[
            pl.BlockSpec((TM, d), lambda i: (i, 0)),
            pl.BlockSpec((TM, d), lambda i: (tile_of(i), 0)),
            pl.BlockSpec((None, 1, mod.shape[-1]), lambda i: row_map(tile_of(i))),
            pl.BlockSpec((d, d), const),
            pl.BlockSpec((1, d), const),
            pl.BlockSpec((1, d), const),
            pl.BlockSpec((d, LANES), const),
            pl.BlockSpec((d, LANES), const),
            pl.BlockSpec((1, LANES), const),
        ],
        out_specs=[
            pl.BlockSpec((TM, d), lambda i: (i, 0)),
            pl.BlockSpec((TM, d), lambda i: (i, 0)),
            pl.BlockSpec((TM, LANES), lambda i: (i, 0)),
            pl.BlockSpec((None, 1, LANES), lambda i: (i, 0, 0)),
        ],
        out_shape=[
            jax.ShapeDtypeStruct((n_tiles * TM, d), F32),
            jax.ShapeDtypeStruct((n_tiles * TM, d), F32),
            jax.ShapeDtypeStruct((n_tiles * TM, LANES), F32),
            jax.ShapeDtypeStruct((n_tiles, 1, LANES), F32),
        ],
        compiler_params=_cparams(("arbitrary",)),
        name="proj_route",
    )(o, xs, mod, wo, ln_g, ln_b, wr_hi, wr_lo, br)


def _dispatch_plan(counts, max_tiles):
    n = counts[:, 0, :N_EXPERTS].astype(jnp.int32)
    units = (n + CHUNK - 1) // CHUNK
    local_off = jnp.cumsum(units, axis=1) - units
    total = jnp.sum(units, axis=0)
    tiles_e = (total + UNITS_PER_TILE - 1) // UNITS_PER_TILE
    tile_end = jnp.cumsum(tiles_e)
    region_off = (tile_end - tiles_e) * UNITS_PER_TILE
    base = region_off[None, :] + jnp.cumsum(units, axis=0) - units
    n_tiles = tile_end[-1:]
    tile_ids = jnp.arange(max_tiles, dtype=jnp.int32)
    tile_expert = jnp.sum((tile_end[None, :] <= tile_ids[:, None]).astype(jnp.int32), axis=1)
    tile_expert = jnp.minimum(tile_expert, N_EXPERTS - 1)
    tail_units = tiles_e * UNITS_PER_TILE - total
    tail_off = region_off + total
    local_off_rows = jnp.zeros((n.shape[0], 1, LANES), F32)
    local_off_rows = local_off_rows.at[:, 0, :N_EXPERTS].set((local_off * CHUNK).astype(F32))
    i32 = lambda a: a.reshape(-1).astype(jnp.int32)
    return dict(units=i32(units), local_off=i32(local_off), base=i32(base), tail_units=i32(tail_units),
                tail_off=i32(tail_off), n_tiles=i32(n_tiles), tile_expert=i32(tile_expert),
                local_off_rows=local_off_rows)


def _for_each_run(units_ref, local_ref, base_ref, tile, fn):
    def body(e, carry):
        idx = tile * N_EXPERTS + e
        n = units_ref[idx]
        lo = local_ref[idx]
        go = base_ref[idx]
        for bit in reversed(range(CHUNK_BITS)):
            @pl.when(((n >> bit) & 1) == 1)
            def _():
                done = (n >> (bit + 1)) << (bit + 1)
                fn(pl.multiple_of((lo + done) * CHUNK, CHUNK), pl.multiple_of((go + done) * CHUNK, CHUNK),
                   (1 << bit) * CHUNK)
        return carry
    lax.fori_loop(0, N_EXPERTS, body, 0)


def _local_positions(route, local_off_rows):
    lane = lax.broadcasted_iota(jnp.int32, route.shape, 1).astype(F32)
    pos = []
    for k in range(2):
        onehot = (lane == route[:, ROUTE_E + k:ROUTE_E + k + 1]).astype(F32)
        off = jnp.sum(onehot * local_off_rows, axis=-1, keepdims=True)
        pos.append(off + route[:, ROUTE_RANK + k:ROUTE_RANK + k + 1])
    return pos


def _selection(pos):
    slot = lax.broadcasted_iota(jnp.int32, (pos.shape[0], LOCAL_ROWS), 1).astype(F32)
    return slot == pos


def _split3(w):
    hi = w.astype(BF16)
    r1 = w - hi.astype(F32)
    mid = r1.astype(BF16)
    lo = (r1 - mid.astype(F32)).astype(BF16)
    return hi, mid, lo


def _dispatch_kernel(units_ref, local_ref, base_ref, tailn_ref, tailoff_ref, nt_ref,
                     h2_ref, route_ref, loff_ref, xs_hbm, buf, zbuf, sems, zsem, *, d, n_tok_tiles, max_tiles):
    t = pl.program_id(0)
    slot = t % 2
    route = route_ref[...]
    pos1, pos2 = _local_positions(route, loff_ref[...])
    sel1 = _selection(pos1)
    sel2 = _selection(pos2)
    contract0 = (((0,), (0,)), ((), ()))
    sel = (sel1 | sel2).astype(BF16)
    buf[slot, :, 0:d] = lax.dot_general(sel, h2_ref[...].astype(BF16), contract0, preferred_element_type=F32)

    lane = lax.broadcasted_iota(jnp.int32, route.shape, 1)
    gate_rows = jnp.zeros((LOCAL_ROWS, LANES), F32)
    for k, selk in ((0, sel1), (1, sel2)):
        pieces = _split3(route[:, ROUTE_W + k:ROUTE_W + k + 1])
        wp = jnp.zeros(route.shape, F32)
        for j, piece in enumerate(pieces):
            wp = jnp.where(lane == j, piece.astype(F32), wp)
        gate_rows += lax.dot_general(selk.astype(BF16), wp.astype(BF16), contract0,
                                     preferred_element_type=F32)
    buf[slot, :, d:] = gate_rows

    def push(s):
        def fn(local_row, global_row, rows):
            pltpu.make_async_copy(buf.at[s, pl.ds(local_row, rows), :],
                                  xs_hbm.at[pl.ds(global_row, rows), :], sems.at[s]).start()
        return fn

    def push_wait(s):
        def fn(local_row, global_row, rows):
            pltpu.make_async_copy(buf.at[s, pl.ds(local_row, rows), :],
                                  xs_hbm.at[pl.ds(global_row, rows), :], sems.at[s]).wait()
        return fn

    _for_each_run(units_ref, local_ref, base_ref, t, push(slot))

    @pl.when(t >= 1)
    def _():
        _for_each_run(units_ref, local_ref, base_ref, t - 1, push_wait(1 - slot))

    @pl.when(t == n_tok_tiles - 1)
    def _():
        _for_each_run(units_ref, local_ref, base_ref, t, push_wait(slot))
        zbuf[...] = jnp.zeros_like(zbuf)

        def tail_copy(e, bit):
            n = tailn_ref[e]
            done = (n >> (bit + 1)) << (bit + 1)
            row = pl.multiple_of((tailoff_ref[e] + done) * CHUNK, CHUNK)
            rows = (1 << bit) * CHUNK
            return pltpu.make_async_copy(zbuf.at[pl.ds(0, rows), :], xs_hbm.at[pl.ds(row, rows), :], zsem)

        def tile_copy(i):
            row = pl.multiple_of(i * TM, TM)
            return pltpu.make_async_copy(zbuf, xs_hbm.at[pl.ds(row, TM), :], zsem)

        for wait in (False, True):
            def tails(e, carry, wait=wait):
                for bit in reversed(range(CHUNK_BITS - 1)):
                    @pl.when(((tailn_ref[e] >> bit) & 1) == 1)
                    def _():
                        cp = tail_copy(e, bit)
                        cp.wait() if wait else cp.start()
                return carry
            lax.fori_loop(0, N_EXPERTS, tails, 0)

            def unused(i, carry, wait=wait):
                cp = tile_copy(i)
                cp.wait() if wait else cp.start()
                return carry
            lax.fori_loop(nt_ref[0], max_tiles, unused, 0)


def _dispatch(h2, route, plan, max_tiles):
    rows, d = h2.shape
    n_tok_tiles = rows // TM
    width = d + LANES
    return pl.pallas_call(
        functools.partial(_dispatch_kernel, d=d, n_tok_tiles=n_tok_tiles, max_tiles=max_tiles),
        grid_spec=pltpu.PrefetchScalarGridSpec(
            num_scalar_prefetch=6,
            grid=(n_tok_tiles,),
            in_specs=[
                pl.BlockSpec((TM, d), lambda t, *_: (t, 0)),
                pl.BlockSpec((TM, LANES), lambda t, *_: (t, 0)),
                pl.BlockSpec((None, 1, LANES), lambda t, *_: (t, 0, 0)),
            ],
            out_specs=pl.BlockSpec(memory_space=pl.ANY),
            scratch_shapes=[
                pltpu.VMEM((2, LOCAL_ROWS, width), F32),
                pltpu.VMEM((TM, width), F32),
                pltpu.SemaphoreType.DMA((2,)),
                pltpu.SemaphoreType.DMA(()),
            ],
        ),
        out_shape=jax.ShapeDtypeStruct((max_tiles * TM, width), F32),
        compiler_params=_cparams(("arbitrary",)),
        name="dispatch",
    )(plan["units"], plan["local_off"], plan["base"], plan["tail_units"], plan["tail_off"], plan["n_tiles"],
      h2, route, plan["local_off_rows"])


def _expert_kernel(te_ref, nt_ref, xs_ref, wg_ref, wu_ref, wd_ref, y_ref, wgu_b, wd_b, *, d, hidden):
    t = pl.program_id(0)
    nt = nt_ref[0]

    @pl.when(t < nt)
    def _():
        prev = te_ref[jnp.maximum(t - 1, 0)]

        @pl.when((t == 0) | (te_ref[t] != prev))
        def _():
            wgu_b[:, 0:hidden] = wg_ref[...].astype(BF16)
            wgu_b[:, hidden:2 * hidden] = wu_ref[...].astype(BF16)
            wd_b[...] = wd_ref[...].astype(BF16)

        x = xs_ref[:, 0:d].astype(BF16)
        gate = xs_ref[:, d:d + 1] + xs_ref[:, d + 1:d + 2] + xs_ref[:, d + 2:d + 3]
        au = jnp.dot(x, wgu_b[...], preferred_element_type=F32)
        a = au[:, 0:hidden]
        u = au[:, hidden:2 * hidden]
        act = (a * jax.nn.sigmoid(a) * u * gate).astype(BF16)
        y_ref[...] = jnp.dot(act, wd_b[...], preferred_element_type=F32)

    @pl.when(t >= nt)
    def _():
        y_ref[...] = jnp.zeros_like(y_ref)


def _experts(xs, plan, w_gate, w_up, w_down, max_tiles):
    width = xs.shape[1]
    d = width - LANES
    hidden = w_gate.shape[-1]
    last = lambda t, te, nt: jnp.minimum(t, nt[0] - 1)
    return pl.pallas_call(
        functools.partial(_expert_kernel, d=d, hidden=hidden),
        grid_spec=pltpu.PrefetchScalarGridSpec(
            num_scalar_prefetch=2,
            grid=(max_tiles,),
            in_specs=[
                pl.BlockSpec((TM, width), lambda t, te, nt: (last(t, te, nt), 0)),
                pl.BlockSpec((None, d, hidden), lambda t, te, nt: (te[last(t, te, nt)], 0, 0)),
                pl.BlockSpec((None, d, hidden), lambda t, te, nt: (te[last(t, te, nt)], 0, 0)),
                pl.BlockSpec((None, hidden, d), lambda t, te, nt: (te[last(t, te, nt)], 0, 0)),
            ],
            out_specs=pl.BlockSpec((TM, d), lambda t, te, nt: (t, 0)),
            scratch_shapes=[
                pltpu.VMEM((d, 2 * hidden), BF16),
                pltpu.VMEM((hidden, d), BF16),
            ],
        ),
        out_shape=jax.ShapeDtypeStruct((max_tiles * TM, d), F32),
        compiler_params=_cparams(("arbitrary",)),
        name="experts",
    )(plan["tile_expert"], plan["n_tiles"], xs, w_gate, w_up, w_down)


def _combine_kernel(units_ref, local_ref, base_ref, y_hbm, x1_ref, route_ref, loff_ref, mod_ref, lng_ref,
                    lnb_ref, o_ref, ybuf, sems, *, d, n_tok_tiles):
    t = pl.program_id(0)
    slot = t % 2

    def pull(s, wait):
        def fn(local_row, global_row, rows):
            cp = pltpu.make_async_copy(y_hbm.at[pl.ds(global_row, rows), :],
                                       ybuf.at[s, pl.ds(local_row, rows), :], sems.at[s])
            cp.wait() if wait else cp.start()
        return fn

    @pl.when(t == 0)
    def _():
        ybuf[...] = jnp.zeros_like(ybuf)
        _for_each_run(units_ref, local_ref, base_ref, 0, pull(0, False))

    @pl.when(t + 1 < n_tok_tiles)
    def _():
        _for_each_run(units_ref, local_ref, base_ref, t + 1, pull(1 - slot, False))

    _for_each_run(units_ref, local_ref, base_ref, t, pull(slot, True))

    pos1, pos2 = _local_positions(route_ref[...], loff_ref[...])
    sel = (_selection(pos1) | _selection(pos2)).astype(BF16)
    y = ybuf[slot]
    y_hi = y.astype(BF16)
    y_lo = (y - y_hi.astype(F32)).astype(BF16)
    fx = jnp.dot(sel, y_hi, preferred_element_type=F32) + jnp.dot(sel, y_lo, preferred_element_type=F32)
    gate_f = mod_ref[:, 5 * d:6 * d]
    o_ref[...] = _layer_norm(DEEPNORM_ALPHA * x1_ref[...] + gate_f * fx, lng_ref[...], lnb_ref[...])


def _combine(ys, plan, x1, route, mod, ln_g, ln_b, mod_tile_of):
    rows, d = x1.shape
    n_tok_tiles = rows // TM
    const = lambda t, *_: (0, 0)
    return pl.pallas_call(
        functools.partial(_combine_kernel, d=d, n_tok_tiles=n_tok_tiles),
        grid_spec=pltpu.PrefetchScalarGridSpec(
            num_scalar_prefetch=3,
            grid=(n_tok_tiles,),
            in_specs=[
                pl.BlockSpec(memory_space=pl.ANY),
                pl.BlockSpec((TM, d), lambda t, *_: (t, 0)),
                pl.BlockSpec((TM, LANES), lambda t, *_: (t, 0)),
                pl.BlockSpec((None, 1, LANES), lambda t, *_: (t, 0, 0)),
                pl.BlockSpec((None, 1, mod.shape[-1]), lambda t, *_: mod_tile_of(t)),
                pl.BlockSpec((1, d), const),
                pl.BlockSpec((1, d), const),
            ],
            out_specs=pl.BlockSpec((TM, d), lambda t, *_: (t, 0)),
            scratch_shapes=[
                pltpu.VMEM((2, LOCAL_ROWS, d), F32),
                pltpu.SemaphoreType.DMA((2,)),
            ],
        ),
        out_shape=jax.ShapeDtypeStruct((rows, d), F32),
        compiler_params=_cparams(("arbitrary",)),
        name="combine",
    )(plan["units"], plan["local_off"], plan["base"], ys, x1, route, plan["local_off_rows"], mod, ln_g, ln_b)


def kernel(x, c, ctx, c_ctx, w_mod, b_mod, ln_mix_g, ln_mix_b, ln_ffn_g, ln_ffn_b, diff_w_qkv, diff_w_o, diff_lambda_q1, diff_lambda_k1, diff_lambda_q2, diff_lambda_k2, diff_subln_g, gqa_w_qkv, gqa_w_o, gqa_q_norm_g, gqa_k_norm_g, moe_w_group, moe_b_group, moe_w_router, moe_b_router, moe_w_gate, moe_w_up, moe_w_down):
    batch, n, d = x.shape
    n_ctx = ctx.shape[1]
    t = n_ctx + n
    assert n % TM == 0 and n_ctx % TM == 0 and n % GRID_W == 0 and d % LANES == 0
    assert w_mod.shape[0] == DEPTH
    tiles_per_sample = t // TM
    ctx_tiles = n_ctx // TM
    q_tiles = tiles_per_sample - ctx_tiles

    pad = (-(batch + 1)) % SUBLANES
    cond = jnp.concatenate([c, c_ctx[None, :], jnp.zeros((pad, d), F32)], axis=0)
    ctx_row = batch
    mod_all = _modulation(cond, w_mod, b_mod)
    mod_all = mod_all.reshape(DEPTH, cond.shape[0], 1, N_MOD * d)

    xs = jnp.concatenate([ctx, x], axis=1).reshape(batch * t, d)

    for i in range(DEPTH):
        last = i == DEPTH - 1
        mod = mod_all[i]
        j = i // 2
        lng_m, lnb_m = ln_mix_g[i][None, :], ln_mix_b[i][None, :]
        lng_f, lnb_f = ln_ffn_g[i][None, :], ln_ffn_b[i][None, :]
        if i % 2 == 0:
            lambda_init = 0.8 - 0.6 * math.exp(-0.3 * i)
            tab = _rope_tables(n_ctx, n, DIFF_HEAD_DIM)
            q, k, v = _qkv_proj("diff", xs, mod, diff_w_qkv[j].astype(BF16), tab, None,
                                tiles_per_sample, ctx_tiles, ctx_row)
            lam_vecs = jnp.stack([diff_lambda_q1[j], diff_lambda_k1[j], diff_lambda_q2[j], diff_lambda_k2[j]])
            o = _diff_attention(q, k, v, lam_vecs.astype(F32), diff_subln_g[j][None, :], batch, t, n_ctx,
                                lambda_init)
            if last:
                o = o.reshape(batch, t, d)[:, n_ctx:].reshape(batch * n, d)
            wo = diff_w_o[j]
        else:
            tab = _rope_tables(n_ctx, n, GQA_HEAD_DIM)
            q, k, v = _qkv_proj("gqa", xs, mod, gqa_w_qkv[j].astype(BF16), tab,
                                (gqa_q_norm_g[j][None, :], gqa_k_norm_g[j][None, :]),
                                tiles_per_sample, ctx_tiles, ctx_row)
            o = _gqa_attention(q, k, v, batch, t, n_ctx)
            if not last:
                raise NotImplementedError("grouped-query layer with context outputs")
            wo = gqa_w_o[j]

        w_r = jnp.zeros((d, LANES), F32)
        w_r = w_r.at[:, :N_EXPERTS].set(moe_w_router[i]).at[:, GROUP_LANE0:GROUP_LANE0 + N_GROUPS].set(moe_w_group[i])
        b_r = jnp.zeros((1, LANES), F32)
        b_r = b_r.at[0, :N_EXPERTS].set(moe_b_router[i]).at[0, GROUP_LANE0:GROUP_LANE0 + N_GROUPS].set(moe_b_group[i])
        wr_hi = w_r.astype(BF16)
        wr_lo = (w_r - wr_hi.astype(F32)).astype(BF16)

        x1, h2, route, counts = _proj_route(o, xs, mod, wo.astype(BF16), lng_m, lnb_m, wr_hi, wr_lo, b_r,
                                            tiles_per_sample, ctx_tiles, ctx_row, skip_ctx=last)
        n_tok_tiles = x1.shape[0] // TM
        max_tiles = (2 * n_tok_tiles * TM + n_tok_tiles * N_EXPERTS * (CHUNK - 1)) // TM + N_EXPERTS
        plan = _dispatch_plan(counts, max_tiles)
        xsorted = _dispatch(h2, route, plan, max_tiles)
        w_gate = moe_w_gate[i].reshape(N_EXPERTS, d, -1)
        w_up = moe_w_up[i].reshape(N_EXPERTS, d, -1)
        w_down = moe_w_down[i].reshape(N_EXPERTS, -1, d)
        ys = _experts(xsorted, plan, w_gate, w_up, w_down, max_tiles)

        row_map = _mod_row_map(tiles_per_sample, ctx_tiles, ctx_row)
        if last:
            mod_tile_of = lambda tt: row_map((tt // q_tiles) * tiles_per_sample + ctx_tiles + tt % q_tiles)
        else:
            mod_tile_of = row_map
        xs = _combine(ys, plan, x1, route, mod, lng_f, lnb_f, mod_tile_of)

    return xs.reshape(batch, n, d)
```

```python
import functools
import math

import jax
import jax.numpy as jnp
from jax import lax
from jax.experimental import pallas as pl
from jax.experimental.pallas import tpu as pltpu

F32 = jnp.float32
BF16 = jnp.bfloat16

GRID_W = 64
DIFF_HEAD_DIM = 64
GQA_HEAD_DIM = 128
GQA_KV_HEADS = 2
ROPE_THETA = 10000.0
N_GROUPS = 4
EXPERTS_PER_GROUP = 8
N_EXPERTS = N_GROUPS * EXPERTS_PER_GROUP
N_MOD = 6
LN_EPS = 1e-5
RMS_EPS = 1e-6
DEPTH = 2
DEEPNORM_ALPHA = (2 * DEPTH) ** 0.25
LOG2E = 1.4426950408889634

LANES = 128
SUBLANES = 8
TM = 256
ATTN_KC = 768
VMEM_LIMIT = 48 * 1024 * 1024

GROUP_LANE0 = N_EXPERTS

CHUNK = SUBLANES
UNITS_PER_TILE = TM // CHUNK
CHUNK_BITS = UNITS_PER_TILE.bit_length()
LOCAL_ROWS = 2 * TM + N_EXPERTS * CHUNK
ROUTE_E, ROUTE_W, ROUTE_RANK = 0, 2, 4


def _cparams(sem):
    return pltpu.CompilerParams(dimension_semantics=sem, vmem_limit_bytes=VMEM_LIMIT)


def _mod_kernel(c_ref, w_ref, b_ref, o_ref):
    c = c_ref[...]
    s = c * jax.nn.sigmoid(c)
    w = w_ref[...]
    sh = s.astype(BF16)
    sl = (s - sh.astype(F32)).astype(BF16)
    wh = w.astype(BF16)
    wl = (w - wh.astype(F32)).astype(BF16)
    acc = jnp.dot(sh, wh, preferred_element_type=F32)
    acc += jnp.dot(sl, wh, preferred_element_type=F32)
    acc += jnp.dot(sh, wl, preferred_element_type=F32)
    o_ref[...] = acc + b_ref[...]


def _modulation(cond, w_mod, b_mod):
    depth, d, width = w_mod.shape
    r = cond.shape[0]
    tn = 512
    return pl.pallas_call(
        _mod_kernel,
        grid=(depth, width // tn),
        in_specs=[
            pl.BlockSpec((r, d), lambda i, j: (0, 0)),
            pl.BlockSpec((None, d, tn), lambda i, j: (i, 0, j)),
            pl.BlockSpec((None, 1, tn), lambda i, j: (i, 0, j)),
        ],
        out_specs=pl.BlockSpec((None, r, tn), lambda i, j: (i, 0, j)),
        out_shape=jax.ShapeDtypeStruct((depth, r, width), F32),
        compiler_params=_cparams(("arbitrary", "arbitrary")),
        name="modulation",
    )(cond, w_mod, b_mod.reshape(depth, 1, width))


def _rope_tables(n_ctx, n, head_dim):
    rows = n // GRID_W
    row = jnp.broadcast_to(jnp.arange(rows, dtype=F32)[:, None], (rows, GRID_W)).reshape(-1)
    col = jnp.broadcast_to(jnp.arange(GRID_W, dtype=F32)[None, :], (rows, GRID_W)).reshape(-1)
    axis_dim = head_dim // 2
    inv_freq = ROPE_THETA ** (-jnp.arange(0, axis_dim, 2, dtype=F32) / axis_dim)
    ang = jnp.stack([row, col], axis=-1)[:, :, None] * inv_freq
    cos, sin = jnp.cos(ang), jnp.sin(ang)
    zero = jnp.zeros_like(sin)
    c = jnp.concatenate([cos, cos], axis=-1).reshape(n, head_dim)
    s_first = jnp.concatenate([-sin, zero], axis=-1).reshape(n, head_dim)
    s_second = jnp.concatenate([zero, sin], axis=-1).reshape(n, head_dim)
    tab = jnp.stack([c, s_first, s_second])
    tab = jnp.tile(tab, (1, 1, LANES // head_dim))
    ident = jnp.stack([jnp.ones((n_ctx, LANES), F32), jnp.zeros((n_ctx, LANES), F32),
                       jnp.zeros((n_ctx, LANES), F32)])
    return jnp.concatenate([ident, tab], axis=1)


def _rope_chunk(x, c, s_first, s_second, quarter):
    return x * c + pltpu.roll(x, LANES - quarter, 1) * s_first + pltpu.roll(x, quarter, 1) * s_second


def _qkv_diff_kernel(x_ref, mod_ref, w_ref, tab_ref, q_ref, k_ref, v_ref, *, d, qscale):
    x = x_ref[...]
    shift = mod_ref[:, 0:d]
    scale = mod_ref[:, d:2 * d]
    h = (x * (1.0 + scale) + shift).astype(BF16)
    qkv = jnp.dot(h, w_ref[...], preferred_element_type=F32)
    c, s_first, s_second = tab_ref[0], tab_ref[1], tab_ref[2]
    quarter = DIFF_HEAD_DIM // 4
    for j in range(d // LANES):
        lo, hi = j * LANES, (j + 1) * LANES
        q = _rope_chunk(qkv[:, lo:hi], c, s_first, s_second, quarter)
        q_ref[:, lo:hi] = (q * qscale).astype(BF16)
        k = _rope_chunk(qkv[:, d + lo:d + hi], c, s_first, s_second, quarter)
        k_ref[:, lo:hi] = k.astype(BF16)
    v_ref[...] = qkv[:, 2 * d:].astype(BF16)


def _rms_head(x, g):
    return x * lax.rsqrt(jnp.mean(x * x, axis=-1, keepdims=True) + RMS_EPS) * g


def _qkv_gqa_kernel(x_ref, mod_ref, w_ref, tab_ref, qg_ref, kg_ref, q_ref, k_ref, v_ref, *, d, qscale):
    x = x_ref[...]
    shift = mod_ref[:, 0:d]
    scale = mod_ref[:, d:2 * d]
    h = (x * (1.0 + scale) + shift).astype(BF16)
    qkv = jnp.dot(h, w_ref[...], preferred_element_type=F32)
    c, s_first, s_second = tab_ref[0], tab_ref[1], tab_ref[2]
    quarter = GQA_HEAD_DIM // 4
    kv_w = GQA_KV_HEADS * GQA_HEAD_DIM
    for j in range(d // LANES):
        lo, hi = j * LANES, (j + 1) * LANES
        q = _rope_chunk(_rms_head(qkv[:, lo:hi], qg_ref[...]), c, s_first, s_second, quarter)
        q_ref[:, lo:hi] = (q * qscale).astype(BF16)
    for j in range(GQA_KV_HEADS):
        lo, hi = j * LANES, (j + 1) * LANES
        k = _rope_chunk(_rms_head(qkv[:, d + lo:d + hi], kg_ref[...]), c, s_first, s_second, quarter)
        k_ref[:, lo:hi] = k.astype(BF16)
    v_ref[...] = qkv[:, d + kv_w:].astype(BF16)


def _mod_row_map(tiles_per_sample, ctx_tiles, ctx_row):
    def index(i):
        b = i // tiles_per_sample
        j = i % tiles_per_sample
        return (jnp.where(j < ctx_tiles, ctx_row, b), 0, 0)
    return index


def _qkv_proj(kind, xs, mod, w, tab, norm_g, tiles_per_sample, ctx_tiles, ctx_row):
    rows, d = xs.shape
    n_tiles = rows // TM
    width = w.shape[1]
    row_map = _mod_row_map(tiles_per_sample, ctx_tiles, ctx_row)
    in_specs = [
        pl.BlockSpec((TM, d), lambda i: (i, 0)),
        pl.BlockSpec((None, 1, mod.shape[-1]), row_map),
        pl.BlockSpec((d, width), lambda i: (0, 0)),
        pl.BlockSpec((3, TM, LANES), lambda i: (0, i % tiles_per_sample, 0)),
    ]
    args = [xs, mod, w, tab]
    if kind == "diff":
        kern = functools.partial(_qkv_diff_kernel, d=d, qscale=DIFF_HEAD_DIM ** -0.5 * LOG2E)
        kw, vw = d, d
    else:
        kern = functools.partial(_qkv_gqa_kernel, d=d, qscale=GQA_HEAD_DIM ** -0.5 * LOG2E)
        kw = vw = GQA_KV_HEADS * GQA_HEAD_DIM
        in_specs += [pl.BlockSpec((1, LANES), lambda i: (0, 0))] * 2
        args += list(norm_g)
    return pl.pallas_call(
        kern,
        grid=(n_tiles,),
        in_specs=in_specs,
        out_specs=[
            pl.BlockSpec((TM, d), lambda i: (i, 0)),
            pl.BlockSpec((TM, kw), lambda i: (i, 0)),
            pl.BlockSpec((TM, vw), lambda i: (i, 0)),
        ],
        out_shape=[
            jax.ShapeDtypeStruct((rows, d), BF16),
            jax.ShapeDtypeStruct((rows, kw), BF16),
            jax.ShapeDtypeStruct((rows, vw), BF16),
        ],
        compiler_params=_cparams(("arbitrary",)),
        name="qkv_" + kind,
    )(*args)


def _flash(q, k_ref, va_ref, n_keys):
    m = None
    acc = None
    for lo in range(0, n_keys, ATTN_KC):
        hi = min(lo + ATTN_KC, n_keys)
        s = lax.dot_general(q, k_ref[lo:hi, :], (((1,), (1,)), ((), ())), preferred_element_type=F32)
        m_new = jnp.max(s, axis=-1, keepdims=True)
        if m is not None:
            m_new = jnp.maximum(m, m_new)
        p = jnp.exp2(s - m_new).astype(BF16)
        pv = jnp.dot(p, va_ref[lo:hi, :], preferred_element_type=F32)
        acc = pv if acc is None else jnp.exp2(m - m_new) * acc + pv
        m = m_new
    return acc[:, 0:LANES] / acc[:, LANES:LANES + 1]


def _fill_values(v_ref, va_ref):
    va_ref[:, 0:LANES] = v_ref[...]
    lane = lax.broadcasted_iota(jnp.int32, v_ref.shape, 1)
    va_ref[:, LANES:2 * LANES] = jnp.where(lane == 0, 1.0, 0.0).astype(va_ref.dtype)


def _diff_attn_kernel(lam_ref, g_ref, q_ref, k_ref, v_ref, o_ref, va_ref, *, n_ctx, ctx_tiles, lambda_init):
    qi = pl.program_id(2)

    @pl.when(qi == 0)
    def _():
        _fill_values(v_ref, va_ref)

    lv = lam_ref[...]
    lam = (jnp.exp(jnp.sum(lv[0:1] * lv[1:2], axis=-1, keepdims=True))
           - jnp.exp(jnp.sum(lv[2:3] * lv[3:4], axis=-1, keepdims=True)) + lambda_init)
    q = q_ref[...]
    rows = q.shape[0]
    lane = lax.broadcasted_iota(jnp.int32, q.shape, 1)
    zero = jnp.zeros_like(q)
    q12 = jnp.concatenate([jnp.where(lane < DIFF_HEAD_DIM, q, zero),
                           jnp.where(lane >= DIFF_HEAD_DIM, q, zero)], axis=0)

    def attend(n_keys):
        o12 = _flash(q12, k_ref, va_ref, n_keys)
        o = o12[0:rows] - lam * o12[rows:2 * rows]
        o = o * lax.rsqrt(jnp.mean(o * o, axis=-1, keepdims=True) + RMS_EPS) * g_ref[...]
        o_ref[...] = (o * (1.0 - lambda_init)).astype(o_ref.dtype)

    @pl.when(qi < ctx_tiles)
    def _():
        attend(n_ctx)

    @pl.when(qi >= ctx_tiles)
    def _():
        attend(k_ref.shape[0])


def _diff_attention(q, k, v, lam_vecs, subln_g, batch, t, n_ctx, lambda_init):
    rows, d = q.shape
    heads = d // LANES
    tiles = t // TM
    kern = functools.partial(_diff_attn_kernel, n_ctx=n_ctx, ctx_tiles=n_ctx // TM, lambda_init=lambda_init)
    return pl.pallas_call(
        kern,
        grid=(batch, heads, tiles),
        in_specs=[
            pl.BlockSpec(lam_vecs.shape, lambda b, h, i: (0, 0)),
            pl.BlockSpec((1, LANES), lambda b, h, i: (0, 0)),
            pl.BlockSpec((TM, LANES), lambda b, h, i: (b * tiles + i, h)),
            pl.BlockSpec((t, LANES), lambda b, h, i: (b, h)),
            pl.BlockSpec((t, LANES), lambda b, h, i: (b, h)),
        ],
        out_specs=pl.BlockSpec((TM, LANES), lambda b, h, i: (b * tiles + i, h)),
        out_shape=jax.ShapeDtypeStruct((rows, d), BF16),
        scratch_shapes=[pltpu.VMEM((t, 2 * LANES), BF16)],
        compiler_params=_cparams(("arbitrary", "arbitrary", "arbitrary")),
        name="diff_attention",
    )(lam_vecs, subln_g, q, k, v)


def _gqa_attn_kernel(q_ref, k_ref, v_ref, o_ref, va_ref, *, group):
    @pl.when(pl.program_id(2) == 0)
    def _():
        _fill_values(v_ref, va_ref)

    rows = q_ref.shape[0]
    n_keys = k_ref.shape[0]
    for g in range(0, group, 2):
        q2 = jnp.concatenate([q_ref[:, g * LANES:(g + 1) * LANES],
                              q_ref[:, (g + 1) * LANES:(g + 2) * LANES]], axis=0)
        o2 = _flash(q2, k_ref, va_ref, n_keys)
        o_ref[:, g * LANES:(g + 1) * LANES] = o2[0:rows].astype(o_ref.dtype)
        o_ref[:, (g + 1) * LANES:(g + 2) * LANES] = o2[rows:2 * rows].astype(o_ref.dtype)


def _gqa_attention(q, k, v, batch, t, n_ctx):
    rows, d = q.shape
    group = d // GQA_HEAD_DIM // GQA_KV_HEADS
    tiles = t // TM
    ctx_tiles = n_ctx // TM
    q_tiles = tiles - ctx_tiles
    gw = group * LANES
    return pl.pallas_call(
        functools.partial(_gqa_attn_kernel, group=group),
        grid=(batch, GQA_KV_HEADS, q_tiles),
        in_specs=[
            pl.BlockSpec((TM, gw), lambda b, h, i: (b * tiles + ctx_tiles + i, h)),
            pl.BlockSpec((t, LANES), lambda b, h, i: (b, h)),
            pl.BlockSpec((t, LANES), lambda b, h, i: (b, h)),
        ],
        out_specs=pl.BlockSpec((TM, gw), lambda b, h, i: (b * q_tiles + i, h)),
        out_shape=jax.ShapeDtypeStruct((batch * q_tiles * TM, d), BF16),
        scratch_shapes=[pltpu.VMEM((t, 2 * LANES), BF16)],
        compiler_params=_cparams(("arbitrary", "arbitrary", "arbitrary")),
        name="gqa_attention",
    )(q, k, v)


def _layer_norm(y, g, b):
    mu = jnp.mean(y, axis=-1, keepdims=True)
    yc = y - mu
    var = jnp.mean(yc * yc, axis=-1, keepdims=True)
    return yc * lax.rsqrt(var + LN_EPS) * g + b


def _proj_route_kernel(o_ref, x_ref, mod_ref, wo_ref, lng_ref, lnb_ref, wrh_ref, wrl_ref, br_ref,
                       x1_ref, h2_ref, route_ref, cnt_ref, *, d):
    gate_m = mod_ref[:, 2 * d:3 * d]
    shift_f = mod_ref[:, 3 * d:4 * d]
    scale_f = mod_ref[:, 4 * d:5 * d]
    ox = jnp.dot(o_ref[...], wo_ref[...], preferred_element_type=F32)
    x1 = _layer_norm(DEEPNORM_ALPHA * x_ref[...] + gate_m * ox, lng_ref[...], lnb_ref[...])
    x1_ref[...] = x1
    h2 = x1 * (1.0 + scale_f) + shift_f
    h2_ref[...] = h2

    hh = h2.astype(BF16)
    hl = (h2 - hh.astype(F32)).astype(BF16)
    logits = jnp.dot(hh, wrh_ref[...], preferred_element_type=F32)
    logits += jnp.dot(hl, wrh_ref[...], preferred_element_type=F32)
    logits += jnp.dot(hh, wrl_ref[...], preferred_element_type=F32)
    logits += br_ref[...]

    lane = lax.broadcasted_iota(jnp.int32, logits.shape, 1).astype(F32)
    neg = jnp.full_like(logits, -jnp.inf)
    big = jnp.full_like(logits, 1e9)
    is_group = (lane >= GROUP_LANE0) & (lane < GROUP_LANE0 + N_GROUPS)
    lg = jnp.where(is_group, logits, neg)
    g_max = jnp.max(lg, axis=-1, keepdims=True)
    g_idx = jnp.min(jnp.where(lg == g_max, lane - GROUP_LANE0, big), axis=-1, keepdims=True)
    g_top = 1.0 / jnp.sum(jnp.exp(lg - g_max), axis=-1, keepdims=True)

    lane_group = jnp.floor(lane * (1.0 / EXPERTS_PER_GROUP))
    in_group = (lane < N_EXPERTS) & (lane_group == g_idx)
    le = jnp.where(in_group, logits, neg)
    m1 = jnp.max(le, axis=-1, keepdims=True)
    i1 = jnp.min(jnp.where(le == m1, lane, big), axis=-1, keepdims=True)
    le2 = jnp.where(lane == i1, neg, le)
    m2 = jnp.max(le2, axis=-1, keepdims=True)
    i2 = jnp.min(jnp.where(le2 == m2, lane, big), axis=-1, keepdims=True)
    r = jnp.exp(m2 - m1)
    w1 = g_top / (1.0 + r)
    w2 = g_top * r / (1.0 + r)

    a1 = (lane == i1).astype(F32)
    a2 = (lane == i2).astype(F32)
    both = (a1 + a2).astype(BF16)
    tm = logits.shape[0]
    rr = lax.broadcasted_iota(jnp.int32, (tm, tm), 0)
    cc = lax.broadcasted_iota(jnp.int32, (tm, tm), 1)
    strict_lower = (rr > cc).astype(BF16)
    before = jnp.dot(strict_lower, both, preferred_element_type=F32)
    rank1 = jnp.sum(a1 * before, axis=-1, keepdims=True)
    rank2 = jnp.sum(a2 * before, axis=-1, keepdims=True)
    cnt_ref[...] = jnp.sum(a1 + a2, axis=0, keepdims=True)

    out = jnp.zeros_like(logits)
    for idx, val in ((ROUTE_E, i1), (ROUTE_E + 1, i2), (ROUTE_W, w1), (ROUTE_W + 1, w2),
                     (ROUTE_RANK, rank1), (ROUTE_RANK + 1, rank2)):
        out = jnp.where(lane == float(idx), val, out)
    route_ref[...] = out


def _proj_route(o, xs, mod, wo, ln_g, ln_b, wr_hi, wr_lo, br, tiles_per_sample, ctx_tiles, ctx_row,
                skip_ctx):
    rows, d = xs.shape
    n_tiles = o.shape[0] // TM
    q_tiles = tiles_per_sample - ctx_tiles
    if skip_ctx:
        tile_of = lambda i: (i // q_tiles) * tiles_per_sample + ctx_tiles + i % q_tiles
    else:
        tile_of = lambda i: i
    row_map = _mod_row_map(tiles_per_sample, ctx_tiles, ctx_row)
    const = lambda i: (0, 0)
    return pl.pallas_call(
        functools.partial(_proj_route_kernel, d=d),
        grid=(n_tiles,),
        in_specs=[
            pl.BlockSpec((TM, d), lambda i: (i, 0)),
            pl.BlockSpec((TM, d), lambda i: (tile_of(i), 0)),
            pl.BlockSpec((None, 1, mod.shape[-1]), lambda i: row_map(tile_of(i))),
            pl.BlockSpec((d, d), const),
            pl.BlockSpec((1, d), const),
            pl.BlockSpec((1, d), const),
            pl.BlockSpec((d, LANES), const),
            pl.BlockSpec((d, LANES), const),
            pl.BlockSpec((1, LANES), const),
        ],
        out_specs=[
            pl.BlockSpec((TM, d), lambda i: (i, 0)),
            pl.BlockSpec((TM, d), lambda i: (i, 0)),
            pl.BlockSpec((TM, LANES), lambda i: (i, 0)),
            pl.BlockSpec((None, 1, LANES), lambda i: (i, 0, 0)),
        ],
        out_shape=[
            jax.ShapeDtypeStruct((n_tiles * TM, d), F32),
            jax.ShapeDtypeStruct((n_tiles * TM, d), F32),
            jax.ShapeDtypeStruct((n_tiles * TM, LANES), F32),
            jax.ShapeDtypeStruct((n_tiles, 1, LANES), F32),
        ],
        compiler_params=_cparams(("arbitrary",)),
        name="proj_route",
    )(o, xs, mod, wo, ln_g, ln_b, wr_hi, wr_lo, br)


def _dispatch_plan(counts, max_tiles):
    n = counts[:, 0, :N_EXPERTS].astype(jnp.int32)
    units = (n + CHUNK - 1) // CHUNK
    local_off = jnp.cumsum(units, axis=1) - units
    total = jnp.sum(units, axis=0)
    tiles_e = (total + UNITS_PER_TILE - 1) // UNITS_PER_TILE
    tile_end = jnp.cumsum(tiles_e)
    region_off = (tile_end - tiles_e) * UNITS_PER_TILE
    base = region_off[None, :] + jnp.cumsum(units, axis=0) - units
    n_tiles = tile_end[-1:]
    tile_ids = jnp.arange(max_tiles, dtype=jnp.int32)
    tile_expert = jnp.sum((tile_end[None, :] <= tile_ids[:, None]).astype(jnp.int32), axis=1)
    tile_expert = jnp.minimum(tile_expert, N_EXPERTS - 1)
    tail_units = tiles_e * UNITS_PER_TILE - total
    tail_off = region_off + total
    local_off_rows = jnp.zeros((n.shape[0], 1, LANES), F32)
    local_off_rows = local_off_rows.at[:, 0, :N_EXPERTS].set((local_off * CHUNK).astype(F32))
    i32 = lambda a: a.reshape(-1).astype(jnp.int32)
    return dict(units=i32(units), local_off=i32(local_off), base=i32(base), tail_units=i32(tail_units),
                tail_off=i32(tail_off), n_tiles=i32(n_tiles), tile_expert=i32(tile_expert),
                local_off_rows=local_off_rows)


def _for_each_run(units_ref, local_ref, base_ref, tile, fn):
    def body(e, carry):
        idx = tile * N_EXPERTS + e
        n = units_ref[idx]
        lo = local_ref[idx]
        go = base_ref[idx]
        for bit in reversed(range(CHUNK_BITS)):
            @pl.when(((n >> bit) & 1) == 1)
            def _():
                done = (n >> (bit + 1)) << (bit + 1)
                fn(pl.multiple_of((lo + done) * CHUNK, CHUNK), pl.multiple_of((go + done) * CHUNK, CHUNK),
                   (1 << bit) * CHUNK)
        return carry
    lax.fori_loop(0, N_EXPERTS, body, 0)


def _local_positions(route, local_off_rows):
    lane = lax.broadcasted_iota(jnp.int32, route.shape, 1).astype(F32)
    pos = []
    for k in range(2):
        onehot = (lane == route[:, ROUTE_E + k:ROUTE_E + k + 1]).astype(F32)
        off = jnp.sum(onehot * local_off_rows, axis=-1, keepdims=True)
        pos.append(off + route[:, ROUTE_RANK + k:ROUTE_RANK + k + 1])
    return pos


def _selection(pos):
    slot = lax.broadcasted_iota(jnp.int32, (pos.shape[0], LOCAL_ROWS), 1).astype(F32)
    return slot == pos


def _split3(w):
    hi = w.astype(BF16)
    r1 = w - hi.astype(F32)
    mid = r1.astype(BF16)
    lo = (r1 - mid.astype(F32)).astype(BF16)
    return hi, mid, lo


def _dispatch_kernel(units_ref, local_ref, base_ref, tailn_ref, tailoff_ref, nt_ref,
                     h2_ref, route_ref, loff_ref, xs_hbm, buf, zbuf, sems, zsem, *, d, n_tok_tiles, max_tiles):
    t = pl.program_id(0)
    slot = t % 2
    route = route_ref[...]
    pos1, pos2 = _local_positions(route, loff_ref[...])
    sel1 = _selection(pos1)
    sel2 = _selection(pos2)
    contract0 = (((0,), (0,)), ((), ()))
    sel = (sel1 | sel2).astype(BF16)
    buf[slot, :, 0:d] = lax.dot_general(sel, h2_ref[...].astype(BF16), contract0, preferred_element_type=F32)

    lane = lax.broadcasted_iota(jnp.int32, route.shape, 1)
    gate_rows = jnp.zeros((LOCAL_ROWS, LANES), F32)
    for k, selk in ((0, sel1), (1, sel2)):
        pieces = _split3(route[:, ROUTE_W + k:ROUTE_W + k + 1])
        wp = jnp.zeros(route.shape, F32)
        for j, piece in enumerate(pieces):
            wp = jnp.where(lane == j, piece.astype(F32), wp)
        gate_rows += lax.dot_general(selk.astype(BF16), wp.astype(BF16), contract0,
                                     preferred_element_type=F32)
    buf[slot, :, d:] = gate_rows

    def push(s):
        def fn(local_row, global_row, rows):
            pltpu.make_async_copy(buf.at[s, pl.ds(local_row, rows), :],
                                  xs_hbm.at[pl.ds(global_row, rows), :], sems.at[s]).start()
        return fn

    def push_wait(s):
        def fn(local_row, global_row, rows):
            pltpu.make_async_copy(buf.at[s, pl.ds(local_row, rows), :],
                                  xs_hbm.at[pl.ds(global_row, rows), :], sems.at[s]).wait()
        return fn

    _for_each_run(units_ref, local_ref, base_ref, t, push(slot))

    @pl.when(t >= 1)
    def _():
        _for_each_run(units_ref, local_ref, base_ref, t - 1, push_wait(1 - slot))

    @pl.when(t == n_tok_tiles - 1)
    def _():
        _for_each_run(units_ref, local_ref, base_ref, t, push_wait(slot))
        zbuf[...] = jnp.zeros_like(zbuf)

        def tail_copy(e, bit):
            n = tailn_ref[e]
            done = (n >> (bit + 1)) << (bit + 1)
            row = pl.multiple_of((tailoff_ref[e] + done) * CHUNK, CHUNK)
            rows = (1 << bit) * CHUNK
            return pltpu.make_async_copy(zbuf.at[pl.ds(0, rows), :], xs_hbm.at[pl.ds(row, rows), :], zsem)

        def tile_copy(i):
            row = pl.multiple_of(i * TM, TM)
            return pltpu.make_async_copy(zbuf, xs_hbm.at[pl.ds(row, TM), :], zsem)

        for wait in (False, True):
            def tails(e, carry, wait=wait):
                for bit in reversed(range(CHUNK_BITS - 1)):
                    @pl.when(((tailn_ref[e] >> bit) & 1) == 1)
                    def _():
                        cp = tail_copy(e, bit)
                        cp.wait() if wait else cp.start()
                return carry
            lax.fori_loop(0, N_EXPERTS, tails, 0)

            def unused(i, carry, wait=wait):
                cp = tile_copy(i)
                cp.wait() if wait else cp.start()
                return carry
            lax.fori_loop(nt_ref[0], max_tiles, unused, 0)


def _dispatch(h2, route, plan, max_tiles):
    rows, d = h2.shape
    n_tok_tiles = rows // TM
    width = d + LANES
    return pl.pallas_call(
        functools.partial(_dispatch_kernel, d=d, n_tok_tiles=n_tok_tiles, max_tiles=max_tiles),
        grid_spec=pltpu.PrefetchScalarGridSpec(
            num_scalar_prefetch=6,
            grid=(n_tok_tiles,),
            in_specs=[
                pl.BlockSpec((TM, d), lambda t, *_: (t, 0)),
                pl.BlockSpec((TM, LANES), lambda t, *_: (t, 0)),
                pl.BlockSpec((None, 1, LANES), lambda t, *_: (t, 0, 0)),
            ],
            out_specs=pl.BlockSpec(memory_space=pl.ANY),
            scratch_shapes=[
                pltpu.VMEM((2, LOCAL_ROWS, width), F32),
                pltpu.VMEM((TM, width), F32),
                pltpu.SemaphoreType.DMA((2,)),
                pltpu.SemaphoreType.DMA(()),
            ],
        ),
        out_shape=jax.ShapeDtypeStruct((max_tiles * TM, width), F32),
        compiler_params=_cparams(("arbitrary",)),
        name="dispatch",
    )(plan["units"], plan["local_off"], plan["base"], plan["tail_units"], plan["tail_off"], plan["n_tiles"],
      h2, route, plan["local_off_rows"])


def _expert_kernel(te_ref, nt_ref, xs_ref, wg_ref, wu_ref, wd_ref, y_ref, wgu_b, wd_b, *, d, hidden):
    t = pl.program_id(0)
    nt = nt_ref[0]

    @pl.when(t < nt)
    def _():
        prev = te_ref[jnp.maximum(t - 1, 0)]

        @pl.when((t == 0) | (te_ref[t] != prev))
        def _():
            wgu_b[:, 0:hidden] = wg_ref[...].astype(BF16)
            wgu_b[:, hidden:2 * hidden] = wu_ref[...].astype(BF16)
            wd_b[...] = wd_ref[...].astype(BF16)

        x = xs_ref[:, 0:d].astype(BF16)
        gate = xs_ref[:, d:d + 1] + xs_ref[:, d + 1:d + 2] + xs_ref[:, d + 2:d + 3]
        au = jnp.dot(x, wgu_b[...], preferred_element_type=F32)
        a = au[:, 0:hidden]
        u = au[:, hidden:2 * hidden]
        act = (a * jax.nn.sigmoid(a) * u * gate).astype(BF16)
        y_ref[...] = jnp.dot(act, wd_b[...], preferred_element_type=F32)

    @pl.when(t >= nt)
    def _():
        y_ref[...] = jnp.zeros_like(y_ref)


def _experts(xs, plan, w_gate, w_up, w_down, max_tiles):
    width = xs.shape[1]
    d = width - LANES
    hidden = w_gate.shape[-1]
    last = lambda t, te, nt: jnp.minimum(t, nt[0] - 1)
    return pl.pallas_call(
        functools.partial(_expert_kernel, d=d, hidden=hidden),
        grid_spec=pltpu.PrefetchScalarGridSpec(
            num_scalar_prefetch=2,
            grid=(max_tiles,),
            in_specs=[
                pl.BlockSpec((TM, width), lambda t, te, nt: (last(t, te, nt), 0)),
                pl.BlockSpec((None, d, hidden), lambda t, te, nt: (te[last(t, te, nt)], 0, 0)),
                pl.BlockSpec((None, d, hidden), lambda t, te, nt: (te[last(t, te, nt)], 0, 0)),
                pl.BlockSpec((None, hidden, d), lambda t, te, nt: (te[last(t, te, nt)], 0, 0)),
            ],
            out_specs=pl.BlockSpec((TM, d), lambda t, te, nt: (t, 0)),
            scratch_shapes=[
                pltpu.VMEM((d, 2 * hidden), BF16),
                pltpu.VMEM((hidden, d), BF16),
            ],
        ),
        out_shape=jax.ShapeDtypeStruct((max_tiles * TM, d), F32),
        compiler_params=_cparams(("arbitrary",)),
        name="experts",
    )(plan["tile_expert"], plan["n_tiles"], xs, w_gate, w_up, w_down)


def _combine_kernel(units_ref, local_ref, base_ref, y_hbm, x1_ref, route_ref, loff_ref, mod_ref, lng_ref,
                    lnb_ref, o_ref, ybuf, sems, *, d, n_tok_tiles):
    t = pl.program_id(0)
    slot = t % 2

    def pull(s, wait):
        def fn(local_row, global_row, rows):
            cp = pltpu.make_async_copy(y_hbm.at[pl.ds(global_row, rows), :],
                                       ybuf.at[s, pl.ds(local_row, rows), :], sems.at[s])
            cp.wait() if wait else cp.start()
        return fn

    @pl.when(t == 0)
    def _():
        ybuf[...] = jnp.zeros_like(ybuf)
        _for_each_run(units_ref, local_ref, base_ref, 0, pull(0, False))

    @pl.when(t + 1 < n_tok_tiles)
    def _():
        _for_each_run(units_ref, local_ref, base_ref, t + 1, pull(1 - slot, False))

    _for_each_run(units_ref, local_ref, base_ref, t, pull(slot, True))

    pos1, pos2 = _local_positions(route_ref[...], loff_ref[...])
    sel = (_selection(pos1) | _selection(pos2)).astype(BF16)
    y = ybuf[slot]
    y_hi = y.astype(BF16)
    y_lo = (y - y_hi.astype(F32)).astype(BF16)
    fx = jnp.dot(sel, y_hi, preferred_element_type=F32) + jnp.dot(sel, y_lo, preferred_element_type=F32)
    gate_f = mod_ref[:, 5 * d:6 * d]
    o_ref[...] = _layer_norm(DEEPNORM_ALPHA * x1_ref[...] + gate_f * fx, lng_ref[...], lnb_ref[...])


def _combine(ys, plan, x1, route, mod, ln_g, ln_b, mod_tile_of):
    rows, d = x1.shape
    n_tok_tiles = rows // TM
    const = lambda t, *_: (0, 0)
    return pl.pallas_call(
        functools.partial(_combine_kernel, d=d, n_tok_tiles=n_tok_tiles),
        grid_spec=pltpu.PrefetchScalarGridSpec(
            num_scalar_prefetch=3,
            grid=(n_tok_tiles,),
            in_specs=[
                pl.BlockSpec(memory_space=pl.ANY),
                pl.BlockSpec((TM, d), lambda t, *_: (t, 0)),
                pl.BlockSpec((TM, LANES), lambda t, *_: (t, 0)),
                pl.BlockSpec((None, 1, LANES), lambda t, *_: (t, 0, 0)),
                pl.BlockSpec((None, 1, mod.shape[-1]), lambda t, *_: mod_tile_of(t)),
                pl.BlockSpec((1, d), const),
                pl.BlockSpec((1, d), const),
            ],
            out_specs=pl.BlockSpec((TM, d), lambda t, *_: (t, 0)),
            scratch_shapes=[
                pltpu.VMEM((2, LOCAL_ROWS, d), F32),
                pltpu.SemaphoreType.DMA((2,)),
            ],
        ),
        out_shape=jax.ShapeDtypeStruct((rows, d), F32),
        compiler_params=_cparams(("arbitrary",)),
        name="combine",
    )(plan["units"], plan["local_off"], plan["base"], ys, x1, route, plan["local_off_rows"], mod, ln_g, ln_b)


def kernel(x, c, ctx, c_ctx, w_mod, b_mod, ln_mix_g, ln_mix_b, ln_ffn_g, ln_ffn_b, diff_w_qkv, diff_w_o, diff_lambda_q1, diff_lambda_k1, diff_lambda_q2, diff_lambda_k2, diff_subln_g, gqa_w_qkv, gqa_w_o, gqa_q_norm_g, gqa_k_norm_g, moe_w_group, moe_b_group, moe_w_router, moe_b_router, moe_w_gate, moe_w_up, moe_w_down):
    batch, n, d = x.shape
    n_ctx = ctx.shape[1]
    t = n_ctx + n
    assert n % TM == 0 and n_ctx % TM == 0 and n % GRID_W == 0 and d % LANES == 0
    assert w_mod.shape[0] == DEPTH
    tiles_per_sample = t // TM
    ctx_tiles = n_ctx // TM
    q_tiles = tiles_per_sample - ctx_tiles

    pad = (-(batch + 1)) % SUBLANES
    cond = jnp.concatenate([c, c_ctx[None, :], jnp.zeros((pad, d), F32)], axis=0)
    ctx_row = batch
    mod_all = _modulation(cond, w_mod, b_mod)
    mod_all = mod_all.reshape(DEPTH, cond.shape[0], 1, N_MOD * d)

    xs = jnp.concatenate([ctx, x], axis=1).reshape(batch * t, d)

    for i in range(DEPTH):
        last = i == DEPTH - 1
        mod = mod_all[i]
        j = i // 2
        lng_m, lnb_m = ln_mix_g[i][None, :], ln_mix_b[i][None, :]
        lng_f, lnb_f = ln_ffn_g[i][None, :], ln_ffn_b[i][None, :]
        if i % 2 == 0:
            lambda_init = 0.8 - 0.6 * math.exp(-0.3 * i)
            tab = _rope_tables(n_ctx, n, DIFF_HEAD_DIM)
            q, k, v = _qkv_proj("diff", xs, mod, diff_w_qkv[j].astype(BF16), tab, None,
                                tiles_per_sample, ctx_tiles, ctx_row)
            lam_vecs = jnp.stack([diff_lambda_q1[j], diff_lambda_k1[j], diff_lambda_q2[j], diff_lambda_k2[j]])
            o = _diff_attention(q, k, v, lam_vecs.astype(F32), diff_subln_g[j][None, :], batch, t, n_ctx,
                                lambda_init)
            if last:
                o = o.reshape(batch, t, d)[:, n_ctx:].reshape(batch * n, d)
            wo = diff_w_o[j]
        else:
            tab = _rope_tables(n_ctx, n, GQA_HEAD_DIM)
            q, k, v = _qkv_proj("gqa", xs, mod, gqa_w_qkv[j].astype(BF16), tab,
                                (gqa_q_norm_g[j][None, :], gqa_k_norm_g[j][None, :]),
                                tiles_per_sample, ctx_tiles, ctx_row)
            o = _gqa_attention(q, k, v, batch, t, n_ctx)
            if not last:
                raise NotImplementedError("grouped-query layer with context outputs")
            wo = gqa_w_o[j]

        w_r = jnp.zeros((d, LANES), F32)
        w_r = w_r.at[:, :N_EXPERTS].set(moe_w_router[i]).at[:, GROUP_LANE0:GROUP_LANE0 + N_GROUPS].set(moe_w_group[i])
        b_r = jnp.zeros((1, LANES), F32)
        b_r = b_r.at[0, :N_EXPERTS].set(moe_b_router[i]).at[0, GROUP_LANE0:GROUP_LANE0 + N_GROUPS].set(moe_b_group[i])
        wr_hi = w_r.astype(BF16)
        wr_lo = (w_r - wr_hi.astype(F32)).astype(BF16)

        x1, h2, route, counts = _proj_route(o, xs, mod, wo.astype(BF16), lng_m, lnb_m, wr_hi, wr_lo, b_r,
                                            tiles_per_sample, ctx_tiles, ctx_row, skip_ctx=last)
        n_tok_tiles = x1.shape[0] // TM
        max_tiles = (2 * n_tok_tiles * TM + n_tok_tiles * N_EXPERTS * (CHUNK - 1)) // TM + N_EXPERTS
        plan = _dispatch_plan(counts, max_tiles)
        xsorted = _dispatch(h2, route, plan, max_tiles)
        w_gate = moe_w_gate[i].reshape(N_EXPERTS, d, -1)
        w_up = moe_w_up[i].reshape(N_EXPERTS, d, -1)
        w_down = moe_w_down[i].reshape(N_EXPERTS, -1, d)
        ys = _experts(xsorted, plan, w_gate, w_up, w_down, max_tiles)

        row_map = _mod_row_map(tiles_per_sample, ctx_tiles, ctx_row)
        if last:
            mod_tile_of = lambda tt: row_map((tt // q_tiles) * tiles_per_sample + ctx_tiles + tt % q_tiles)
        else:
            mod_tile_of = row_map
        xs = _combine(ys, plan, x1, route, mod, lng_f, lnb_f, mod_tile_of)

    return xs.reshape(batch, n, d)
```

```python
import functools
import math

import jax
import jax.numpy as jnp
from jax import lax
from jax.experimental import pallas as pl
from jax.experimental.pallas import tpu as pltpu

F32 = jnp.float32
BF16 = jnp.bfloat16

GRID_W = 64
DIFF_HEAD_DIM = 64
GQA_HEAD_DIM = 128
GQA_KV_HEADS = 2
ROPE_THETA = 10000.0
N_GROUPS = 4
EXPERTS_PER_GROUP = 8
N_EXPERTS = N_GROUPS * EXPERTS_PER_GROUP
N_MOD = 6
LN_EPS = 1e-5
RMS_EPS = 1e-6
DEPTH = 2
DEEPNORM_ALPHA = (2 * DEPTH) ** 0.25
LOG2E = 1.4426950408889634

LANES = 128
SUBLANES = 8
TM = 256
ATTN_KC = 768
VMEM_LIMIT = 48 * 1024 * 1024

GROUP_LANE0 = N_EXPERTS

CHUNK = SUBLANES
UNITS_PER_TILE = TM // CHUNK
TAIL_BITS = (UNITS_PER_TILE - 1).bit_length()
LOCAL_ROWS = 2 * TM + N_EXPERTS * CHUNK
LOCAL_UNITS = LOCAL_ROWS // CHUNK
ROUTE_E, ROUTE_W, ROUTE_RANK = 0, 2, 4


def _cparams(sem):
    return pltpu.CompilerParams(dimension_semantics=sem, vmem_limit_bytes=VMEM_LIMIT)


def _mod_kernel(c_ref, w_ref, b_ref, o_ref):
    c = c_ref[...]
    s = c * jax.nn.sigmoid(c)
    w = w_ref[...]
    sh = s.astype(BF16)
    sl = (s - sh.astype(F32)).astype(BF16)
    wh = w.astype(BF16)
    wl = (w - wh.astype(F32)).astype(BF16)
    acc = jnp.dot(sh, wh, preferred_element_type=F32)
    acc += jnp.dot(sl, wh, preferred_element_type=F32)
    acc += jnp.dot(sh, wl, preferred_element_type=F32)
    o_ref[...] = acc + b_ref[...]


def _modulation(cond, w_mod, b_mod):
    depth, d, width = w_mod.shape
    r = cond.shape[0]
    tn = 512
    return pl.pallas_call(
        _mod_kernel,
        grid=(depth, width // tn),
        in_specs=[
            pl.BlockSpec((r, d), lambda i, j: (0, 0)),
            pl.BlockSpec((None, d, tn), lambda i, j: (i, 0, j)),
            pl.BlockSpec((None, 1, tn), lambda i, j: (i, 0, j)),
        ],
        out_specs=pl.BlockSpec((None, r, tn), lambda i, j: (i, 0, j)),
        out_shape=jax.ShapeDtypeStruct((depth, r, width), F32),
        compiler_params=_cparams(("arbitrary", "arbitrary")),
        name="modulation",
    )(cond, w_mod, b_mod.reshape(depth, 1, width))


def _rope_tables(n_ctx, n, head_dim):
    rows = n // GRID_W
    row = jnp.broadcast_to(jnp.arange(rows, dtype=F32)[:, None], (rows, GRID_W)).reshape(-1)
    col = jnp.broadcast_to(jnp.arange(GRID_W, dtype=F32)[None, :], (rows, GRID_W)).reshape(-1)
    axis_dim = head_dim // 2
    inv_freq = ROPE_THETA ** (-jnp.arange(0, axis_dim, 2, dtype=F32) / axis_dim)
    ang = jnp.stack([row, col], axis=-1)[:, :, None] * inv_freq
    cos, sin = jnp.cos(ang), jnp.sin(ang)
    zero = jnp.zeros_like(sin)
    c = jnp.concatenate([cos, cos], axis=-1).reshape(n, head_dim)
    s_first = jnp.concatenate([-sin, zero], axis=-1).reshape(n, head_dim)
    s_second = jnp.concatenate([zero, sin], axis=-1).reshape(n, head_dim)
    tab = jnp.stack([c, s_first, s_second])
    tab = jnp.tile(tab, (1, 1, LANES // head_dim))
    ident = jnp.stack([jnp.ones((n_ctx, LANES), F32), jnp.zeros((n_ctx, LANES), F32),
                       jnp.zeros((n_ctx, LANES), F32)])
    return jnp.concatenate([ident, tab], axis=1)


def _rope_chunk(x, c, s_first, s_second, quarter):
    return x * c + pltpu.roll(x, LANES - quarter, 1) * s_first + pltpu.roll(x, quarter, 1) * s_second


def _qkv_diff_kernel(x_ref, mod_ref, w_ref, tab_ref, q_ref, k_ref, v_ref, *, d, qscale):
    x = x_ref[...]
    shift = mod_ref[:, 0:d]
    scale = mod_ref[:, d:2 * d]
    h = (x * (1.0 + scale) + shift).astype(BF16)
    qkv = jnp.dot(h, w_ref[...], preferred_element_type=F32)
    c, s_first, s_second = tab_ref[0], tab_ref[1], tab_ref[2]
    quarter = DIFF_HEAD_DIM // 4
    for j in range(d // LANES):
        lo, hi = j * LANES, (j + 1) * LANES
        q = _rope_chunk(qkv[:, lo:hi], c, s_first, s_second, quarter)
        q_ref[:, lo:hi] = (q * qscale).astype(BF16)
        k = _rope_chunk(qkv[:, d + lo:d + hi], c, s_first, s_second, quarter)
        k_ref[:, lo:hi] = k.astype(BF16)
    v_ref[...] = qkv[:, 2 * d:].astype(BF16)


def _rms_head(x, g):
    return x * lax.rsqrt(jnp.mean(x * x, axis=-1, keepdims=True) + RMS_EPS) * g


def _qkv_gqa_kernel(x_ref, mod_ref, w_ref, tab_ref, qg_ref, kg_ref, q_ref, k_ref, v_ref, *, d, qscale):
    x = x_ref[...]
    shift = mod_ref[:, 0:d]
    scale = mod_ref[:, d:2 * d]
    h = (x * (1.0 + scale) + shift).astype(BF16)
    qkv = jnp.dot(h, w_ref[...], preferred_element_type=F32)
    c, s_first, s_second = tab_ref[0], tab_ref[1], tab_ref[2]
    quarter = GQA_HEAD_DIM // 4
    kv_w = GQA_KV_HEADS * GQA_HEAD_DIM
    for j in range(d // LANES):
        lo, hi = j * LANES, (j + 1) * LANES
        q = _rope_chunk(_rms_head(qkv[:, lo:hi], qg_ref[...]), c, s_first, s_second, quarter)
        q_ref[:, lo:hi] = (q * qscale).astype(BF16)
    for j in range(GQA_KV_HEADS):
        lo, hi = j * LANES, (j + 1) * LANES
        k = _rope_chunk(_rms_head(qkv[:, d + lo:d + hi], kg_ref[...]), c, s_first, s_second, quarter)
        k_ref[:, lo:hi] = k.astype(BF16)
    v_ref[...] = qkv[:, d + kv_w:].astype(BF16)


def _mod_row_map(tiles_per_sample, ctx_tiles, ctx_row):
    def index(i):
        b = i // tiles_per_sample
        j = i % tiles_per_sample
        return (jnp.where(j < ctx_tiles, ctx_row, b), 0, 0)
    return index


def _qkv_proj(kind, xs, mod, w, tab, norm_g, tiles_per_sample, ctx_tiles, ctx_row):
    rows, d = xs.shape
    n_tiles = rows // TM
    width = w.shape[1]
    row_map = _mod_row_map(tiles_per_sample, ctx_tiles, ctx_row)
    in_specs = [
        pl.BlockSpec((TM, d), lambda i: (i, 0)),
        pl.BlockSpec((None, 1, mod.shape[-1]), row_map),
        pl.BlockSpec((d, width), lambda i: (0, 0)),
        pl.BlockSpec((3, TM, LANES), lambda i: (0, i % tiles_per_sample, 0)),
    ]
    args = [xs, mod, w, tab]
    if kind == "diff":
        kern = functools.partial(_qkv_diff_kernel, d=d, qscale=DIFF_HEAD_DIM ** -0.5 * LOG2E)
        kw, vw = d, d
    else:
        kern = functools.partial(_qkv_gqa_kernel, d=d, qscale=GQA_HEAD_DIM ** -0.5 * LOG2E)
        kw = vw = GQA_KV_HEADS * GQA_HEAD_DIM
        in_specs += [pl.BlockSpec((1, LANES), lambda i: (0, 0))] * 2
        args += list(norm_g)
    return pl.pallas_call(
        kern,
        grid=(n_tiles,),
        in_specs=in_specs,
        out_specs=[
            pl.BlockSpec((TM, d), lambda i: (i, 0)),
            pl.BlockSpec((TM, kw), lambda i: (i, 0)),
            pl.BlockSpec((TM, vw), lambda i: (i, 0)),
        ],
        out_shape=[
            jax.ShapeDtypeStruct((rows, d), BF16),
            jax.ShapeDtypeStruct((rows, kw), BF16),
            jax.ShapeDtypeStruct((rows, vw), BF16),
        ],
        compiler_params=_cparams(("arbitrary",)),
        name="qkv_" + kind,
    )(*args)


def _flash(q, k_ref, va_ref, n_keys):
    m = None
    acc = None
    for lo in range(0, n_keys, ATTN_KC):
        hi = min(lo + ATTN_KC, n_keys)
        s = lax.dot_general(q, k_ref[lo:hi, :], (((1,), (1,)), ((), ())), preferred_element_type=F32)
        m_new = jnp.max(s, axis=-1, keepdims=True)
        if m is not None:
            m_new = jnp.maximum(m, m_new)
        p = jnp.exp2(s - m_new).astype(BF16)
        pv = jnp.dot(p, va_ref[lo:hi, :], preferred_element_type=F32)
        acc = pv if acc is None else jnp.exp2(m - m_new) * acc + pv
        m = m_new
    return acc[:, 0:LANES] / acc[:, LANES:LANES + 1]


def _fill_values(v_ref, va_ref):
    va_ref[:, 0:LANES] = v_ref[...]
    lane = lax.broadcasted_iota(jnp.int32, v_ref.shape, 1)
    va_ref[:, LANES:2 * LANES] = jnp.where(lane == 0, 1.0, 0.0).astype(va_ref.dtype)


def _diff_attn_kernel(lam_ref, g_ref, q_ref, k_ref, v_ref, o_ref, va_ref, *, n_ctx, ctx_tiles, lambda_init):
    qi = pl.program_id(2)

    @pl.when(qi == 0)
    def _():
        _fill_values(v_ref, va_ref)

    lv = lam_ref[...]
    lam = (jnp.exp(jnp.sum(lv[0:1] * lv[1:2], axis=-1, keepdims=True))
           - jnp.exp(jnp.sum(lv[2:3] * lv[3:4], axis=-1, keepdims=True)) + lambda_init)
    q = q_ref[...]
    rows = q.shape[0]
    lane = lax.broadcasted_iota(jnp.int32, q.shape, 1)
    zero = jnp.zeros_like(q)
    q12 = jnp.concatenate([jnp.where(lane < DIFF_HEAD_DIM, q, zero),
                           jnp.where(lane >= DIFF_HEAD_DIM, q, zero)], axis=0)

    def attend(n_keys):
        o12 = _flash(q12, k_ref, va_ref, n_keys)
        o = o12[0:rows] - lam * o12[rows:2 * rows]
        o = o * lax.rsqrt(jnp.mean(o * o, axis=-1, keepdims=True) + RMS_EPS) * g_ref[...]
        o_ref[...] = (o * (1.0 - lambda_init)).astype(o_ref.dtype)

    @pl.when(qi < ctx_tiles)
    def _():
        attend(n_ctx)

    @pl.when(qi >= ctx_tiles)
    def _():
        attend(k_ref.shape[0])


def _diff_attention(q, k, v, lam_vecs, subln_g, batch, t, n_ctx, lambda_init):
    rows, d = q.shape
    heads = d // LANES
    tiles = t // TM
    kern = functools.partial(_diff_attn_kernel, n_ctx=n_ctx, ctx_tiles=n_ctx // TM, lambda_init=lambda_init)
    return pl.pallas_call(
        kern,
        grid=(batch, heads, tiles),
        in_specs=[
            pl.BlockSpec(lam_vecs.shape, lambda b, h, i: (0, 0)),
            pl.BlockSpec((1, LANES), lambda b, h, i: (0, 0)),
            pl.BlockSpec((TM, LANES), lambda b, h, i: (b * tiles + i, h)),
            pl.BlockSpec((t, LANES), lambda b, h, i: (b, h)),
            pl.BlockSpec((t, LANES), lambda b, h, i: (b, h)),
        ],
        out_specs=pl.BlockSpec((TM, LANES), lambda b, h, i: (b * tiles + i, h)),
        out_shape=jax.ShapeDtypeStruct((rows, d), BF16),
        scratch_shapes=[pltpu.VMEM((t, 2 * LANES), BF16)],
        compiler_params=_cparams(("arbitrary", "arbitrary", "arbitrary")),
        name="diff_attention",
    )(lam_vecs, subln_g, q, k, v)


def _gqa_attn_kernel(q_ref, k_ref, v_ref, o_ref, va_ref, *, group):
    @pl.when(pl.program_id(2) == 0)
    def _():
        _fill_values(v_ref, va_ref)

    rows = q_ref.shape[0]
    n_keys = k_ref.shape[0]
    for g in range(0, group, 2):
        q2 = jnp.concatenate([q_ref[:, g * LANES:(g + 1) * LANES],
                              q_ref[:, (g + 1) * LANES:(g + 2) * LANES]], axis=0)
        o2 = _flash(q2, k_ref, va_ref, n_keys)
        o_ref[:, g * LANES:(g + 1) * LANES] = o2[0:rows].astype(o_ref.dtype)
        o_ref[:, (g + 1) * LANES:(g + 2) * LANES] = o2[rows:2 * rows].astype(o_ref.dtype)


def _gqa_attention(q, k, v, batch, t, n_ctx):
    rows, d = q.shape
    group = d // GQA_HEAD_DIM // GQA_KV_HEADS
    tiles = t // TM
    ctx_tiles = n_ctx // TM
    q_tiles = tiles - ctx_tiles
    gw = group * LANES
    return pl.pallas_call(
        functools.partial(_gqa_attn_kernel, group=group),
        grid=(batch, GQA_KV_HEADS, q_tiles),
        in_specs=[
            pl.BlockSpec((TM, gw), lambda b, h, i: (b * tiles + ctx_tiles + i, h)),
            pl.BlockSpec((t, LANES), lambda b, h, i: (b, h)),
            pl.BlockSpec((t, LANES), lambda b, h, i: (b, h)),
        ],
        out_specs=pl.BlockSpec((TM, gw), lambda b, h, i: (b * q_tiles + i, h)),
        out_shape=jax.ShapeDtypeStruct((batch * q_tiles * TM, d), BF16),
        scratch_shapes=[pltpu.VMEM((t, 2 * LANES), BF16)],
        compiler_params=_cparams(("arbitrary", "arbitrary", "arbitrary")),
        name="gqa_attention",
    )(q, k, v)


def _layer_norm(y, g, b):
    mu = jnp.mean(y, axis=-1, keepdims=True)
    yc = y - mu
    var = jnp.mean(yc * yc, axis=-1, keepdims=True)
    return yc * lax.rsqrt(var + LN_EPS) * g + b


def _proj_route_kernel(o_ref, x_ref, mod_ref, wo_ref, lng_ref, lnb_ref, wrh_ref, wrl_ref, br_ref,
                       x1_ref, h2_ref, route_ref, cnt_ref, *, d):
    gate_m = mod_ref[:, 2 * d:3 * d]
    shift_f = mod_ref[:, 3 * d:4 * d]
    scale_f = mod_ref[:, 4 * d:5 * d]
    ox = jnp.dot(o_ref[...], wo_ref[...], preferred_element_type=F32)
    x1 = _layer_norm(DEEPNORM_ALPHA * x_ref[...] + gate_m * ox, lng_ref[...], lnb_ref[...])
    x1_ref[...] = x1
    h2 = x1 * (1.0 + scale_f) + shift_f
    h2_ref[...] = h2

    hh = h2.astype(BF16)
    hl = (h2 - hh.astype(F32)).astype(BF16)
    logits = jnp.dot(hh, wrh_ref[...], preferred_element_type=F32)
    logits += jnp.dot(hl, wrh_ref[...], preferred_element_type=F32)
    logits += jnp.dot(hh, wrl_ref[...], preferred_element_type=F32)
    logits += br_ref[...]

    lane = lax.broadcasted_iota(jnp.int32, logits.shape, 1).astype(F32)
    neg = jnp.full_like(logits, -jnp.inf)
    big = jnp.full_like(logits, 1e9)
    is_group = (lane >= GROUP_LANE0) & (lane < GROUP_LANE0 + N_GROUPS)
    lg = jnp.where(is_group, logits, neg)
    g_max = jnp.max(lg, axis=-1, keepdims=True)
    g_idx = jnp.min(jnp.where(lg == g_max, lane - GROUP_LANE0, big), axis=-1, keepdims=True)
    g_top = 1.0 / jnp.sum(jnp.exp(lg - g_max), axis=-1, keepdims=True)

    lane_group = jnp.floor(lane * (1.0 / EXPERTS_PER_GROUP))
    in_group = (lane < N_EXPERTS) & (lane_group == g_idx)
    le = jnp.where(in_group, logits, neg)
    m1 = jnp.max(le, axis=-1, keepdims=True)
    i1 = jnp.min(jnp.where(le == m1, lane, big), axis=-1, keepdims=True)
    le2 = jnp.where(lane == i1, neg, le)
    m2 = jnp.max(le2, axis=-1, keepdims=True)
    i2 = jnp.min(jnp.where(le2 == m2, lane, big), axis=-1, keepdims=True)
    r = jnp.exp(m2 - m1)
    w1 = g_top / (1.0 + r)
    w2 = g_top * r / (1.0 + r)

    a1 = (lane == i1).astype(F32)
    a2 = (lane == i2).astype(F32)
    both = (a1 + a2).astype(BF16)
    tm = logits.shape[0]
    rr = lax.broadcasted_iota(jnp.int32, (tm, tm), 0)
    cc = lax.broadcasted_iota(jnp.int32, (tm, tm), 1)
    strict_lower = (rr > cc).astype(BF16)
    before = jnp.dot(strict_lower, both, preferred_element_type=F32)
    rank1 = jnp.sum(a1 * before, axis=-1, keepdims=True)
    rank2 = jnp.sum(a2 * before, axis=-1, keepdims=True)
    cnt_ref[...] = jnp.sum(a1 + a2, axis=0, keepdims=True)

    out = jnp.zeros_like(logits)
    for idx, val in ((ROUTE_E, i1), (ROUTE_E + 1, i2), (ROUTE_W, w1), (ROUTE_W + 1, w2),
                     (ROUTE_RANK, rank1), (ROUTE_RANK + 1, rank2)):
        out = jnp.where(lane == float(idx), val, out)
    route_ref[...] = out


def _proj_route(o, xs, mod, wo, ln_g, ln_b, wr_hi, wr_lo, br, tiles_per_sample, ctx_tiles, ctx_row,
                skip_ctx):
    rows, d = xs.shape
    n_tiles = o.shape[0] // TM
    q_tiles = tiles_per_sample - ctx_tiles
    if skip_ctx:
        tile_of = lambda i: (i // q_tiles) * tiles_per_sample + ctx_tiles + i % q_tiles
    else:
        tile_of = lambda i: i
    row_map = _mod_row_map(tiles_per_sample, ctx_tiles, ctx_row)
    const = lambda i: (0, 0)
    return pl.pallas_call(
        functools.partial(_proj_route_kernel, d=d),
        grid=(n_tiles,),
        in_specs=[
            pl.BlockSpec((TM, d), lambda i: (i, 0)),
            pl.BlockSpec((TM, d), lambda i: (tile_of(i), 0)),
            pl.BlockSpec((None, 1, mod.shape[-1]), lambda i: row_map(tile_of(i))),
            pl.BlockSpec((d, d), const),
            pl.BlockSpec((1, d), const),
            pl.BlockSpec((1, d), const),
            pl.BlockSpec((d, LANES), const),
            pl.BlockSpec((d, LANES), const),
            pl.BlockSpec((1, LANES), const),
        ],
        out_specs=[
            pl.BlockSpec((TM, d), lambda i: (i, 0)),
            pl.BlockSpec((TM, d), lambda i: (i, 0)),
            pl.BlockSpec((TM, LANES), lambda i: (i, 0)),
            pl.BlockSpec((None, 1, LANES), lambda i: (i, 0, 0)),
        ],
        out_shape=[
            jax.ShapeDtypeStruct((n_tiles * TM, d), F32),
            jax.ShapeDtypeStruct((n_tiles * TM, d), F32),
            jax.ShapeDtypeStruct((n_tiles * TM, LANES), F32),
            jax.ShapeDtypeStruct((n_tiles, 1, LANES), F32),
        ],
        compiler_params=_cparams(("arbitrary",)),
        name="proj_route",
    )(o, xs, mod, wo, ln_g, ln_b, wr_hi, wr_lo, br)


def _dispatch_plan(counts, max_tiles, expert_base):
    n = counts[:, 0, :N_EXPERTS].astype(jnp.int32)
    units = (n + CHUNK - 1) // CHUNK
    local_off = jnp.cumsum(units, axis=1) - units
    total = jnp.sum(units, axis=0)
    tiles_e = (total + UNITS_PER_TILE - 1) // UNITS_PER_TILE
    tile_end = jnp.cumsum(tiles_e)
    region_off = (tile_end - tiles_e) * UNITS_PER_TILE
    base = region_off[None, :] + jnp.cumsum(units, axis=0) - units
    n_tiles = tile_end[-1:]
    tile_ids = jnp.arange(max_tiles, dtype=jnp.int32)
    tile_expert = jnp.sum((tile_end[None, :] <= tile_ids[:, None]).astype(jnp.int32), axis=1)
    tile_expert = jnp.minimum(tile_expert, N_EXPERTS - 1)
    tail_units = tiles_e * UNITS_PER_TILE - total
    tail_off = region_off + total
    local_off_rows = jnp.zeros((n.shape[0], 1, LANES), F32)
    local_off_rows = local_off_rows.at[:, 0, :N_EXPERTS].set((local_off * CHUNK).astype(F32))
    unit_ids = jnp.arange(LOCAL_UNITS, dtype=jnp.int32)
    local_end = local_off + units
    expert_of = jnp.sum((local_end[:, None, :] <= unit_ids[None, :, None]).astype(jnp.int32), axis=-1)
    onehot = (expert_of[:, :, None] == jnp.arange(N_EXPERTS, dtype=jnp.int32)).astype(jnp.int32)
    global_unit = jnp.sum(onehot * (base - local_off)[:, None, :], axis=-1) + unit_ids[None, :]
    i32 = lambda a: a.reshape(-1).astype(jnp.int32)
    return dict(tile_units=i32(jnp.sum(units, axis=1)), global_unit=i32(global_unit),
                tail_units=i32(tail_units), tail_off=i32(tail_off), n_tiles=i32(n_tiles),
                tile_expert=i32(tile_expert) + expert_base, local_off_rows=local_off_rows)


def _for_each_unit(tile_units_ref, global_unit_ref, tile, fn):
    def body(j, carry):
        g = global_unit_ref[tile * LOCAL_UNITS + j]
        fn(pl.multiple_of(j * CHUNK, CHUNK), pl.multiple_of(g * CHUNK, CHUNK))
        return carry
    lax.fori_loop(0, tile_units_ref[tile], body, 0)


def _local_positions(route, local_off_rows):
    lane = lax.broadcasted_iota(jnp.int32, route.shape, 1).astype(F32)
    pos = []
    for k in range(2):
        onehot = (lane == route[:, ROUTE_E + k:ROUTE_E + k + 1]).astype(F32)
        off = jnp.sum(onehot * local_off_rows, axis=-1, keepdims=True)
        pos.append(off + route[:, ROUTE_RANK + k:ROUTE_RANK + k + 1])
    return pos


def _selection(pos):
    slot = lax.broadcasted_iota(jnp.int32, (pos.shape[0], LOCAL_ROWS), 1).astype(F32)
    return slot == pos


def _split3(w):
    hi = w.astype(BF16)
    r1 = w - hi.astype(F32)
    mid = r1.astype(BF16)
    lo = (r1 - mid.astype(F32)).astype(BF16)
    return hi, mid, lo


def _dispatch_kernel(tile_units_ref, global_unit_ref, tailn_ref, tailoff_ref, nt_ref,
                     h2_ref, route_ref, loff_ref, xs_hbm, buf, zbuf, sems, zsem, *, d, n_tok_tiles, max_tiles):
    t = pl.program_id(0)
    slot = t % 2
    route = route_ref[...]
    pos1, pos2 = _local_positions(route, loff_ref[...])
    sel1 = _selection(pos1)
    sel2 = _selection(pos2)
    contract0 = (((0,), (0,)), ((), ()))
    sel = (sel1 | sel2).astype(BF16)
    buf[slot, :, 0:d] = lax.dot_general(sel, h2_ref[...].astype(BF16), contract0, preferred_element_type=F32)

    lane = lax.broadcasted_iota(jnp.int32, route.shape, 1)
    gate_rows = jnp.zeros((LOCAL_ROWS, LANES), F32)
    for k, selk in ((0, sel1), (1, sel2)):
        pieces = _split3(route[:, ROUTE_W + k:ROUTE_W + k + 1])
        wp = jnp.zeros(route.shape, F32)
        for j, piece in enumerate(pieces):
            wp = jnp.where(lane == j, piece.astype(F32), wp)
        gate_rows += lax.dot_general(selk.astype(BF16), wp.astype(BF16), contract0,
                                     preferred_element_type=F32)
    buf[slot, :, d:] = gate_rows

    def push(s, wait):
        def fn(local_row, global_row):
            cp = pltpu.make_async_copy(buf.at[s, pl.ds(local_row, CHUNK), :],
                                       xs_hbm.at[pl.ds(global_row, CHUNK), :], sems.at[s])
            cp.wait() if wait else cp.start()
        return fn

    _for_each_unit(tile_units_ref, global_unit_ref, t, push(slot, False))

    @pl.when(t >= 1)
    def _():
        _for_each_unit(tile_units_ref, global_unit_ref, t - 1, push(1 - slot, True))

    @pl.when(t == n_tok_tiles - 1)
    def _():
        _for_each_unit(tile_units_ref, global_unit_ref, t, push(slot, True))
        zbuf[...] = jnp.zeros_like(zbuf)

        def tail_copy(e, bit):
            n = tailn_ref[e]
            done = (n >> (bit + 1)) << (bit + 1)
            row = pl.multiple_of((tailoff_ref[e] + done) * CHUNK, CHUNK)
            rows = (1 << bit) * CHUNK
            return pltpu.make_async_copy(zbuf.at[pl.ds(0, rows), :], xs_hbm.at[pl.ds(row, rows), :], zsem)

        def tile_copy(i):
            row = pl.multiple_of(i * TM, TM)
            return pltpu.make_async_copy(zbuf, xs_hbm.at[pl.ds(row, TM), :], zsem)

        for wait in (False, True):
            def tails(e, carry, wait=wait):
                for bit in reversed(range(TAIL_BITS)):
                    @pl.when(((tailn_ref[e] >> bit) & 1) == 1)
                    def _():
                        cp = tail_copy(e, bit)
                        cp.wait() if wait else cp.start()
                return carry
            lax.fori_loop(0, N_EXPERTS, tails, 0)

            def unused(i, carry, wait=wait):
                cp = tile_copy(i)
                cp.wait() if wait else cp.start()
                return carry
            lax.fori_loop(nt_ref[0], max_tiles, unused, 0)


def _dispatch(h2, route, plan, max_tiles):
    rows, d = h2.shape
    n_tok_tiles = rows // TM
    width = d + LANES
    return pl.pallas_call(
        functools.partial(_dispatch_kernel, d=d, n_tok_tiles=n_tok_tiles, max_tiles=max_tiles),
        grid_spec=pltpu.PrefetchScalarGridSpec(
            num_scalar_prefetch=5,
            grid=(n_tok_tiles,),
            in_specs=[
                pl.BlockSpec((TM, d), lambda t, *_: (t, 0)),
                pl.BlockSpec((TM, LANES), lambda t, *_: (t, 0)),
                pl.BlockSpec((None, 1, LANES), lambda t, *_: (t, 0, 0)),
            ],
            out_specs=pl.BlockSpec(memory_space=pl.ANY),
            scratch_shapes=[
                pltpu.VMEM((2, LOCAL_ROWS, width), F32),
                pltpu.VMEM((TM, width), F32),
                pltpu.SemaphoreType.DMA((2,)),
                pltpu.SemaphoreType.DMA(()),
            ],
        ),
        out_shape=jax.ShapeDtypeStruct((max_tiles * TM, width), F32),
        compiler_params=_cparams(("arbitrary",)),
        name="dispatch",
    )(plan["tile_units"], plan["global_unit"], plan["tail_units"], plan["tail_off"], plan["n_tiles"],
      h2, route, plan["local_off_rows"])


def _expert_kernel(te_ref, nt_ref, xs_ref, wg_ref, wu_ref, wd_ref, y_ref, wgu_b, wd_b, *, d, hidden):
    t = pl.program_id(0)
    nt = nt_ref[0]

    @pl.when(t < nt)
    def _():
        prev = te_ref[jnp.maximum(t - 1, 0)]

        @pl.when((t == 0) | (te_ref[t] != prev))
        def _():
            wgu_b[:, 0:hidden] = wg_ref[...].astype(BF16)
            wgu_b[:, hidden:2 * hidden] = wu_ref[...].astype(BF16)
            wd_b[...] = wd_ref[...].astype(BF16)

        x = xs_ref[:, 0:d].astype(BF16)
        gate = xs_ref[:, d:d + 1] + xs_ref[:, d + 1:d + 2] + xs_ref[:, d + 2:d + 3]
        au = jnp.dot(x, wgu_b[...], preferred_element_type=F32)
        a = au[:, 0:hidden]
        u = au[:, hidden:2 * hidden]
        act = (a * jax.nn.sigmoid(a) * u * gate).astype(BF16)
        y_ref[...] = jnp.dot(act, wd_b[...], preferred_element_type=F32)

    @pl.when(t >= nt)
    def _():
        y_ref[...] = jnp.zeros_like(y_ref)


def _experts(xs, plan, w_gate, w_up, w_down, max_tiles):
    width = xs.shape[1]
    d = width - LANES
    hidden = w_gate.shape[-1]
    last = lambda t, te, nt: jnp.minimum(t, nt[0] - 1)
    return pl.pallas_call(
        functools.partial(_expert_kernel, d=d, hidden=hidden),
        grid_spec=pltpu.PrefetchScalarGridSpec(
            num_scalar_prefetch=2,
            grid=(max_tiles,),
            in_specs=[
                pl.BlockSpec((TM, width), lambda t, te, nt: (last(t, te, nt), 0)),
                pl.BlockSpec((None, d, hidden), lambda t, te, nt: (te[last(t, te, nt)], 0, 0)),
                pl.BlockSpec((None, d, hidden), lambda t, te, nt: (te[last(t, te, nt)], 0, 0)),
                pl.BlockSpec((None, hidden, d), lambda t, te, nt: (te[last(t, te, nt)], 0, 0)),
            ],
            out_specs=pl.BlockSpec((TM, d), lambda t, te, nt: (t, 0)),
            scratch_shapes=[
                pltpu.VMEM((d, 2 * hidden), BF16),
                pltpu.VMEM((hidden, d), BF16),
            ],
        ),
        out_shape=jax.ShapeDtypeStruct((max_tiles * TM, d), F32),
        compiler_params=_cparams(("arbitrary",)),
        name="experts",
    )(plan["tile_expert"], plan["n_tiles"], xs, w_gate, w_up, w_down)


def _combine_kernel(tile_units_ref, global_unit_ref, y_hbm, x1_ref, route_ref, loff_ref, mod_ref, lng_ref,
                    lnb_ref, o_ref, ybuf, sems, *, d, n_tok_tiles):
    t = pl.program_id(0)
    slot = t % 2

    def pull(s, wait):
        def fn(local_row, global_row):
            cp = pltpu.make_async_copy(y_hbm.at[pl.ds(global_row, CHUNK), :],
                                       ybuf.at[s, pl.ds(local_row, CHUNK), :], sems.at[s])
            cp.wait() if wait else cp.start()
        return fn

    @pl.when(t == 0)
    def _():
        ybuf[...] = jnp.zeros_like(ybuf)
        _for_each_unit(tile_units_ref, global_unit_ref, 0, pull(0, False))

    @pl.when(t + 1 < n_tok_tiles)
    def _():
        _for_each_unit(tile_units_ref, global_unit_ref, t + 1, pull(1 - slot, False))

    _for_each_unit(tile_units_ref, global_unit_ref, t, pull(slot, True))

    pos1, pos2 = _local_positions(route_ref[...], loff_ref[...])
    sel = (_selection(pos1) | _selection(pos2)).astype(BF16)
    y = ybuf[slot]
    y_hi = y.astype(BF16)
    y_lo = (y - y_hi.astype(F32)).astype(BF16)
    fx = jnp.dot(sel, y_hi, preferred_element_type=F32) + jnp.dot(sel, y_lo, preferred_element_type=F32)
    gate_f = mod_ref[:, 5 * d:6 * d]
    o_ref[...] = _layer_norm(DEEPNORM_ALPHA * x1_ref[...] + gate_f * fx, lng_ref[...], lnb_ref[...])


def _combine(ys, plan, x1, route, mod, ln_g, ln_b, mod_tile_of):
    rows, d = x1.shape
    n_tok_tiles = rows // TM
    const = lambda t, *_: (0, 0)
    return pl.pallas_call(
        functools.partial(_combine_kernel, d=d, n_tok_tiles=n_tok_tiles),
        grid_spec=pltpu.PrefetchScalarGridSpec(
            num_scalar_prefetch=2,
            grid=(n_tok_tiles,),
            in_specs=[
                pl.BlockSpec(memory_space=pl.ANY),
                pl.BlockSpec((TM, d), lambda t, *_: (t, 0)),
                pl.BlockSpec((TM, LANES), lambda t, *_: (t, 0)),
                pl.BlockSpec((None, 1, LANES), lambda t, *_: (t, 0, 0)),
                pl.BlockSpec((None, 1, mod.shape[-1]), lambda t, *_: mod_tile_of(t)),
                pl.BlockSpec((1, d), const),
                pl.BlockSpec((1, d), const),
            ],
            out_specs=pl.BlockSpec((TM, d), lambda t, *_: (t, 0)),
            scratch_shapes=[
                pltpu.VMEM((2, LOCAL_ROWS, d), F32),
                pltpu.SemaphoreType.DMA((2,)),
            ],
        ),
        out_shape=jax.ShapeDtypeStruct((rows, d), F32),
        compiler_params=_cparams(("arbitrary",)),
        name="combine",
    )(plan["tile_units"], plan["global_unit"], ys, x1, route, plan["local_off_rows"], mod, ln_g, ln_b)


def kernel(x, c, ctx, c_ctx, w_mod, b_mod, ln_mix_g, ln_mix_b, ln_ffn_g, ln_ffn_b, diff_w_qkv, diff_w_o, diff_lambda_q1, diff_lambda_k1, diff_lambda_q2, diff_lambda_k2, diff_subln_g, gqa_w_qkv, gqa_w_o, gqa_q_norm_g, gqa_k_norm_g, moe_w_group, moe_b_group, moe_w_router, moe_b_router, moe_w_gate, moe_w_up, moe_w_down):
    batch, n, d = x.shape
    n_ctx = ctx.shape[1]
    t = n_ctx + n
    assert n % TM == 0 and n_ctx % TM == 0 and n % GRID_W == 0 and d % LANES == 0
    assert w_mod.shape[0] == DEPTH
    tiles_per_sample = t // TM
    ctx_tiles = n_ctx // TM
    q_tiles = tiles_per_sample - ctx_tiles

    pad = (-(batch + 1)) % SUBLANES
    cond = jnp.concatenate([c, c_ctx[None, :], jnp.zeros((pad, d), F32)], axis=0)
    ctx_row = batch
    mod_all = _modulation(cond, w_mod, b_mod)
    mod_all = mod_all.reshape(DEPTH, cond.shape[0], 1, N_MOD * d)

    xs = jnp.concatenate([ctx, x], axis=1).reshape(batch * t, d)

    for i in range(DEPTH):
        last = i == DEPTH - 1
        mod = mod_all[i]
        j = i // 2
        lng_m, lnb_m = ln_mix_g[i][None, :], ln_mix_b[i][None, :]
        lng_f, lnb_f = ln_ffn_g[i][None, :], ln_ffn_b[i][None, :]
        if i % 2 == 0:
            lambda_init = 0.8 - 0.6 * math.exp(-0.3 * i)
            tab = _rope_tables(n_ctx, n, DIFF_HEAD_DIM)
            q, k, v = _qkv_proj("diff", xs, mod, diff_w_qkv[j].astype(BF16), tab, None,
                                tiles_per_sample, ctx_tiles, ctx_row)
            lam_vecs = jnp.stack([diff_lambda_q1[j], diff_lambda_k1[j], diff_lambda_q2[j], diff_lambda_k2[j]])
            o = _diff_attention(q, k, v, lam_vecs.astype(F32), diff_subln_g[j][None, :], batch, t, n_ctx,
                                lambda_init)
            if last:
                o = o.reshape(batch, t, d)[:, n_ctx:].reshape(batch * n, d)
            wo = diff_w_o[j]
        else:
            tab = _rope_tables(n_ctx, n, GQA_HEAD_DIM)
            q, k, v = _qkv_proj("gqa", xs, mod, gqa_w_qkv[j].astype(BF16), tab,
                                (gqa_q_norm_g[j][None, :], gqa_k_norm_g[j][None, :]),
                                tiles_per_sample, ctx_tiles, ctx_row)
            o = _gqa_attention(q, k, v, batch, t, n_ctx)
            if not last:
                raise NotImplementedError("grouped-query layer with context outputs")
            wo = gqa_w_o[j]

        w_r = jnp.zeros((d, LANES), F32)
        w_r = w_r.at[:, :N_EXPERTS].set(moe_w_router[i]).at[:, GROUP_LANE0:GROUP_LANE0 + N_GROUPS].set(moe_w_group[i])
        b_r = jnp.zeros((1, LANES), F32)
        b_r = b_r.at[0, :N_EXPERTS].set(moe_b_router[i]).at[0, GROUP_LANE0:GROUP_LANE0 + N_GROUPS].set(moe_b_group[i])
        wr_hi = w_r.astype(BF16)
        wr_lo = (w_r - wr_hi.astype(F32)).astype(BF16)

        x1, h2, route, counts = _proj_route(o, xs, mod, wo.astype(BF16), lng_m, lnb_m, wr_hi, wr_lo, b_r,
                                            tiles_per_sample, ctx_tiles, ctx_row, skip_ctx=last)
        n_tok_tiles = x1.shape[0] // TM
        max_tiles = (2 * n_tok_tiles * TM + n_tok_tiles * N_EXPERTS * (CHUNK - 1)) // TM + N_EXPERTS
        plan = _dispatch_plan(counts, max_tiles, expert_base=i * N_EXPERTS)
        xsorted = _dispatch(h2, route, plan, max_tiles)
        hidden = moe_w_gate.shape[-1]
        w_gate = moe_w_gate.reshape(DEPTH * N_EXPERTS, d, hidden)
        w_up = moe_w_up.reshape(DEPTH * N_EXPERTS, d, hidden)
        w_down = moe_w_down.reshape(DEPTH * N_EXPERTS, hidden, d)
        ys = _experts(xsorted, plan, w_gate, w_up, w_down, max_tiles)

        row_map = _mod_row_map(tiles_per_sample, ctx_tiles, ctx_row)
        if last:
            mod_tile_of = lambda tt: row_map((tt // q_tiles) * tiles_per_sample + ctx_tiles + tt % q_tiles)
        else:
            mod_tile_of = row_map
        xs = _combine(ys, plan, x1, route, mod, lng_f, lnb_f, mod_tile_of)

    return xs.reshape(batch, n, d)
```

```python
import functools
import math

import jax
import jax.numpy as jnp
from jax import lax
from jax.experimental import pallas as pl
from jax.experimental.pallas import tpu as pltpu

F32 = jnp.float32
BF16 = jnp.bfloat16

GRID_W = 64
DIFF_HEAD_DIM = 64
GQA_HEAD_DIM = 128
GQA_KV_HEADS = 2
ROPE_THETA = 10000.0
N_GROUPS = 4
EXPERTS_PER_GROUP = 8
N_EXPERTS = N_GROUPS * EXPERTS_PER_GROUP
N_MOD = 6
LN_EPS = 1e-5
RMS_EPS = 1e-6
DEPTH = 2
DEEPNORM_ALPHA = (2 * DEPTH) ** 0.25
LOG2E = 1.4426950408889634

LANES = 128
SUBLANES = 8
TM = 256
TE = 512
ATTN_KC = 1152
VMEM_LIMIT = 48 * 1024 * 1024

GROUP_LANE0 = N_EXPERTS

CHUNK = SUBLANES
UNITS_PER_TILE = TE // CHUNK
TAIL_BITS = (UNITS_PER_TILE - 1).bit_length()
LOCAL_ROWS = 2 * TM + N_EXPERTS * CHUNK
LOCAL_UNITS = LOCAL_ROWS // CHUNK
ROUTE_E, ROUTE_W, ROUTE_RANK = 0, 2, 4


def _cparams(sem):
    return pltpu.CompilerParams(dimension_semantics=sem, vmem_limit_bytes=VMEM_LIMIT)


def _mod_kernel(c_ref, w_ref, b_ref, o_ref):
    c = c_ref[...]
    s = c * jax.nn.sigmoid(c)
    w = w_ref[...]
    sh = s.astype(BF16)
    sl = (s - sh.astype(F32)).astype(BF16)
    wh = w.astype(BF16)
    wl = (w - wh.astype(F32)).astype(BF16)
    acc = jnp.dot(sh, wh, preferred_element_type=F32)
    acc += jnp.dot(sl, wh, preferred_element_type=F32)
    acc += jnp.dot(sh, wl, preferred_element_type=F32)
    o_ref[...] = acc + b_ref[...]


def _modulation(cond, w_mod, b_mod):
    depth, d, width = w_mod.shape
    r = cond.shape[0]
    tn = 512
    return pl.pallas_call(
        _mod_kernel,
        grid=(depth, width // tn),
        in_specs=[
            pl.BlockSpec((r, d), lambda i, j: (0, 0)),
            pl.BlockSpec((None, d, tn), lambda i, j: (i, 0, j)),
            pl.BlockSpec((None, 1, tn), lambda i, j: (i, 0, j)),
        ],
        out_specs=pl.BlockSpec((None, r, tn), lambda i, j: (i, 0, j)),
        out_shape=jax.ShapeDtypeStruct((depth, r, width), F32),
        compiler_params=_cparams(("arbitrary", "arbitrary")),
        name="modulation",
    )(cond, w_mod, b_mod.reshape(depth, 1, width))


def _rope_tables(n_ctx, n, head_dim):
    rows = n // GRID_W
    row = jnp.broadcast_to(jnp.arange(rows, dtype=F32)[:, None], (rows, GRID_W)).reshape(-1)
    col = jnp.broadcast_to(jnp.arange(GRID_W, dtype=F32)[None, :], (rows, GRID_W)).reshape(-1)
    axis_dim = head_dim // 2
    inv_freq = ROPE_THETA ** (-jnp.arange(0, axis_dim, 2, dtype=F32) / axis_dim)
    ang = jnp.stack([row, col], axis=-1)[:, :, None] * inv_freq
    cos, sin = jnp.cos(ang), jnp.sin(ang)
    zero = jnp.zeros_like(sin)
    c = jnp.concatenate([cos, cos], axis=-1).reshape(n, head_dim)
    s_first = jnp.concatenate([-sin, zero], axis=-1).reshape(n, head_dim)
    s_second = jnp.concatenate([zero, sin], axis=-1).reshape(n, head_dim)
    tab = jnp.stack([c, s_first, s_second])
    tab = jnp.tile(tab, (1, 1, LANES // head_dim))
    ident = jnp.stack([jnp.ones((n_ctx, LANES), F32), jnp.zeros((n_ctx, LANES), F32),
                       jnp.zeros((n_ctx, LANES), F32)])
    return jnp.concatenate([ident, tab], axis=1)


def _rope_chunk(x, c, s_first, s_second, quarter):
    return x * c + pltpu.roll(x, LANES - quarter, 1) * s_first + pltpu.roll(x, quarter, 1) * s_second


def _qkv_diff_kernel(x_ref, mod_ref, w_ref, tab_ref, q_ref, k_ref, v_ref, *, d, qscale):
    x = x_ref[...]
    shift = mod_ref[:, 0:d]
    scale = mod_ref[:, d:2 * d]
    h = (x * (1.0 + scale) + shift).astype(BF16)
    qkv = jnp.dot(h, w_ref[...], preferred_element_type=F32)
    c, s_first, s_second = tab_ref[0], tab_ref[1], tab_ref[2]
    quarter = DIFF_HEAD_DIM // 4
    for j in range(d // LANES):
        lo, hi = j * LANES, (j + 1) * LANES
        q = _rope_chunk(qkv[:, lo:hi], c, s_first, s_second, quarter)
        q_ref[:, lo:hi] = (q * qscale).astype(BF16)
        k = _rope_chunk(qkv[:, d + lo:d + hi], c, s_first, s_second, quarter)
        k_ref[:, lo:hi] = k.astype(BF16)
    v_ref[...] = qkv[:, 2 * d:].astype(BF16)


def _rms_head(x, g):
    return x * lax.rsqrt(jnp.mean(x * x, axis=-1, keepdims=True) + RMS_EPS) * g


def _qkv_gqa_kernel(x_ref, mod_ref, w_ref, tab_ref, qg_ref, kg_ref, q_ref, k_ref, v_ref, *, d, qscale):
    x = x_ref[...]
    shift = mod_ref[:, 0:d]
    scale = mod_ref[:, d:2 * d]
    h = (x * (1.0 + scale) + shift).astype(BF16)
    qkv = jnp.dot(h, w_ref[...], preferred_element_type=F32)
    c, s_first, s_second = tab_ref[0], tab_ref[1], tab_ref[2]
    quarter = GQA_HEAD_DIM // 4
    kv_w = GQA_KV_HEADS * GQA_HEAD_DIM
    for j in range(d // LANES):
        lo, hi = j * LANES, (j + 1) * LANES
        q = _rope_chunk(_rms_head(qkv[:, lo:hi], qg_ref[...]), c, s_first, s_second, quarter)
        q_ref[:, lo:hi] = (q * qscale).astype(BF16)
    for j in range(GQA_KV_HEADS):
        lo, hi = j * LANES, (j + 1) * LANES
        k = _rope_chunk(_rms_head(qkv[:, d + lo:d + hi], kg_ref[...]), c, s_first, s_second, quarter)
        k_ref[:, lo:hi] = k.astype(BF16)
    v_ref[...] = qkv[:, d + kv_w:].astype(BF16)


def _mod_row_map(tiles_per_sample, ctx_tiles, ctx_row):
    def index(i):
        b = i // tiles_per_sample
        j = i % tiles_per_sample
        return (jnp.where(j < ctx_tiles, ctx_row, b), 0, 0)
    return index


def _qkv_proj(kind, xs, mod, w, tab, norm_g, tiles_per_sample, ctx_tiles, ctx_row):
    rows, d = xs.shape
    n_tiles = rows // TM
    width = w.shape[1]
    row_map = _mod_row_map(tiles_per_sample, ctx_tiles, ctx_row)
    in_specs = [
        pl.BlockSpec((TM, d), lambda i: (i, 0)),
        pl.BlockSpec((None, 1, mod.shape[-1]), row_map),
        pl.BlockSpec((d, width), lambda i: (0, 0)),
        pl.BlockSpec((3, TM, LANES), lambda i: (0, i % tiles_per_sample, 0)),
    ]
    args = [xs, mod, w, tab]
    if kind == "diff":
        kern = functools.partial(_qkv_diff_kernel, d=d, qscale=DIFF_HEAD_DIM ** -0.5 * LOG2E)
        kw, vw = d, d
    else:
        kern = functools.partial(_qkv_gqa_kernel, d=d, qscale=GQA_HEAD_DIM ** -0.5 * LOG2E)
        kw = vw = GQA_KV_HEADS * GQA_HEAD_DIM
        in_specs += [pl.BlockSpec((1, LANES), lambda i: (0, 0))] * 2
        args += list(norm_g)
    return pl.pallas_call(
        kern,
        grid=(n_tiles,),
        in_specs=in_specs,
        out_specs=[
            pl.BlockSpec((TM, d), lambda i: (i, 0)),
            pl.BlockSpec((TM, kw), lambda i: (i, 0)),
            pl.BlockSpec((TM, vw), lambda i: (i, 0)),
        ],
        out_shape=[
            jax.ShapeDtypeStruct((rows, d), BF16),
            jax.ShapeDtypeStruct((rows, kw), BF16),
            jax.ShapeDtypeStruct((rows, vw), BF16),
        ],
        compiler_params=_cparams(("arbitrary",)),
        name="qkv_" + kind,
    )(*args)


def _flash(q, k_ref, va_ref, n_keys):
    m = None
    acc = None
    for lo in range(0, n_keys, ATTN_KC):
        hi = min(lo + ATTN_KC, n_keys)
        s = lax.dot_general(q, k_ref[lo:hi, :], (((1,), (1,)), ((), ())), preferred_element_type=F32)
        m_new = jnp.max(s, axis=-1, keepdims=True)
        if m is not None:
            m_new = jnp.maximum(m, m_new)
        p = jnp.exp2(s - m_new).astype(BF16)
        pv = jnp.dot(p, va_ref[lo:hi, :], preferred_element_type=F32)
        acc = pv if acc is None else jnp.exp2(m - m_new) * acc + pv
        m = m_new
    return acc[:, 0:LANES] / acc[:, LANES:LANES + 1]


def _fill_values(v_ref, va_ref):
    va_ref[:, 0:LANES] = v_ref[...]
    lane = lax.broadcasted_iota(jnp.int32, v_ref.shape, 1)
    va_ref[:, LANES:2 * LANES] = jnp.where(lane == 0, 1.0, 0.0).astype(va_ref.dtype)


def _diff_attn_kernel(lam_ref, g_ref, q_ref, k_ref, v_ref, o_ref, va_ref, *, n_ctx, ctx_tiles, lambda_init):
    qi = pl.program_id(2)

    @pl.when(qi == 0)
    def _():
        _fill_values(v_ref, va_ref)

    lv = lam_ref[...]
    lam = (jnp.exp(jnp.sum(lv[0:1] * lv[1:2], axis=-1, keepdims=True))
           - jnp.exp(jnp.sum(lv[2:3] * lv[3:4], axis=-1, keepdims=True)) + lambda_init)
    q = q_ref[...]
    rows = q.shape[0]
    lane = lax.broadcasted_iota(jnp.int32, q.shape, 1)
    zero = jnp.zeros_like(q)
    q12 = jnp.concatenate([jnp.where(lane < DIFF_HEAD_DIM, q, zero),
                           jnp.where(lane >= DIFF_HEAD_DIM, q, zero)], axis=0)

    def attend(n_keys):
        o12 = _flash(q12, k_ref, va_ref, n_keys)
        o = o12[0:rows] - lam * o12[rows:2 * rows]
        o = o * lax.rsqrt(jnp.mean(o * o, axis=-1, keepdims=True) + RMS_EPS) * g_ref[...]
        o_ref[...] = (o * (1.0 - lambda_init)).astype(o_ref.dtype)

    @pl.when(qi < ctx_tiles)
    def _():
        attend(n_ctx)

    @pl.when(qi >= ctx_tiles)
    def _():
        attend(k_ref.shape[0])


def _diff_attention(q, k, v, lam_vecs, subln_g, batch, t, n_ctx, lambda_init):
    rows, d = q.shape
    heads = d // LANES
    tiles = t // TM
    kern = functools.partial(_diff_attn_kernel, n_ctx=n_ctx, ctx_tiles=n_ctx // TM, lambda_init=lambda_init)
    return pl.pallas_call(
        kern,
        grid=(batch, heads, tiles),
        in_specs=[
            pl.BlockSpec(lam_vecs.shape, lambda b, h, i: (0, 0)),
            pl.BlockSpec((1, LANES), lambda b, h, i: (0, 0)),
            pl.BlockSpec((TM, LANES), lambda b, h, i: (b * tiles + i, h)),
            pl.BlockSpec((t, LANES), lambda b, h, i: (b, h)),
            pl.BlockSpec((t, LANES), lambda b, h, i: (b, h)),
        ],
        out_specs=pl.BlockSpec((TM, LANES), lambda b, h, i: (b * tiles + i, h)),
        out_shape=jax.ShapeDtypeStruct((rows, d), BF16),
        scratch_shapes=[pltpu.VMEM((t, 2 * LANES), BF16)],
        compiler_params=_cparams(("arbitrary", "arbitrary", "arbitrary")),
        name="diff_attention",
    )(lam_vecs, subln_g, q, k, v)


def _gqa_attn_kernel(q_ref, k_ref, v_ref, o_ref, va_ref, *, group):
    @pl.when(pl.program_id(2) == 0)
    def _():
        _fill_values(v_ref, va_ref)

    rows = q_ref.shape[0]
    n_keys = k_ref.shape[0]
    q_all = jnp.concatenate([q_ref[:, g * LANES:(g + 1) * LANES] for g in range(group)], axis=0)
    o_all = _flash(q_all, k_ref, va_ref, n_keys)
    for g in range(group):
        o_ref[:, g * LANES:(g + 1) * LANES] = o_all[g * rows:(g + 1) * rows].astype(o_ref.dtype)


def _gqa_attention(q, k, v, batch, t, n_ctx):
    rows, d = q.shape
    group = d // GQA_HEAD_DIM // GQA_KV_HEADS
    tiles = t // TM
    ctx_tiles = n_ctx // TM
    q_tiles = tiles - ctx_tiles
    gw = group * LANES
    return pl.pallas_call(
        functools.partial(_gqa_attn_kernel, group=group),
        grid=(batch, GQA_KV_HEADS, q_tiles),
        in_specs=[
            pl.BlockSpec((TM, gw), lambda b, h, i: (b * tiles + ctx_tiles + i, h)),
            pl.BlockSpec((t, LANES), lambda b, h, i: (b, h)),
            pl.BlockSpec((t, LANES), lambda b, h, i: (b, h)),
        ],
        out_specs=pl.BlockSpec((TM, gw), lambda b, h, i: (b * q_tiles + i, h)),
        out_shape=jax.ShapeDtypeStruct((batch * q_tiles * TM, d), BF16),
        scratch_shapes=[pltpu.VMEM((t, 2 * LANES), BF16)],
        compiler_params=_cparams(("arbitrary", "arbitrary", "arbitrary")),
        name="gqa_attention",
    )(q, k, v)


def _layer_norm(y, g, b):
    mu = jnp.mean(y, axis=-1, keepdims=True)
    yc = y - mu
    var = jnp.mean(yc * yc, axis=-1, keepdims=True)
    return yc * lax.rsqrt(var + LN_EPS) * g + b


def _proj_route_kernel(o_ref, x_ref, mod_ref, wo_ref, lng_ref, lnb_ref, wrh_ref, wrl_ref, br_ref,
                       x1_ref, h2_ref, route_ref, cnt_ref, *, d):
    gate_m = mod_ref[:, 2 * d:3 * d]
    shift_f = mod_ref[:, 3 * d:4 * d]
    scale_f = mod_ref[:, 4 * d:5 * d]
    ox = jnp.dot(o_ref[...], wo_ref[...], preferred_element_type=F32)
    x1 = _layer_norm(DEEPNORM_ALPHA * x_ref[...] + gate_m * ox, lng_ref[...], lnb_ref[...])
    x1_ref[...] = x1
    h2 = x1 * (1.0 + scale_f) + shift_f
    h2_ref[...] = h2

    hh = h2.astype(BF16)
    hl = (h2 - hh.astype(F32)).astype(BF16)
    logits = jnp.dot(hh, wrh_ref[...], preferred_element_type=F32)
    logits += jnp.dot(hl, wrh_ref[...], preferred_element_type=F32)
    logits += jnp.dot(hh, wrl_ref[...], preferred_element_type=F32)
    logits += br_ref[...]

    lane = lax.broadcasted_iota(jnp.int32, logits.shape, 1).astype(F32)
    neg = jnp.full_like(logits, -jnp.inf)
    big = jnp.full_like(logits, 1e9)
    is_group = (lane >= GROUP_LANE0) & (lane < GROUP_LANE0 + N_GROUPS)
    lg = jnp.where(is_group, logits, neg)
    g_max = jnp.max(lg, axis=-1, keepdims=True)
    g_idx = jnp.min(jnp.where(lg == g_max, lane - GROUP_LANE0, big), axis=-1, keepdims=True)
    g_top = 1.0 / jnp.sum(jnp.exp(lg - g_max), axis=-1, keepdims=True)

    lane_group = jnp.floor(lane * (1.0 / EXPERTS_PER_GROUP))
    in_group = (lane < N_EXPERTS) & (lane_group == g_idx)
    le = jnp.where(in_group, logits, neg)
    m1 = jnp.max(le, axis=-1, keepdims=True)
    i1 = jnp.min(jnp.where(le == m1, lane, big), axis=-1, keepdims=True)
    le2 = jnp.where(lane == i1, neg, le)
    m2 = jnp.max(le2, axis=-1, keepdims=True)
    i2 = jnp.min(jnp.where(le2 == m2, lane, big), axis=-1, keepdims=True)
    r = jnp.exp(m2 - m1)
    w1 = g_top / (1.0 + r)
    w2 = g_top * r / (1.0 + r)

    a1 = (lane == i1).astype(F32)
    a2 = (lane == i2).astype(F32)
    both = (a1 + a2).astype(BF16)
    tm = logits.shape[0]
    rr = lax.broadcasted_iota(jnp.int32, (tm, tm), 0)
    cc = lax.broadcasted_iota(jnp.int32, (tm, tm), 1)
    strict_lower = (rr > cc).astype(BF16)
    before = jnp.dot(strict_lower, both, preferred_element_type=F32)
    rank1 = jnp.sum(a1 * before, axis=-1, keepdims=True)
    rank2 = jnp.sum(a2 * before, axis=-1, keepdims=True)
    cnt_ref[...] = jnp.sum(a1 + a2, axis=0, keepdims=True)

    out = jnp.zeros_like(logits)
    for idx, val in ((ROUTE_E, i1), (ROUTE_E + 1, i2), (ROUTE_W, w1), (ROUTE_W + 1, w2),
                     (ROUTE_RANK, rank1), (ROUTE_RANK + 1, rank2)):
        out = jnp.where(lane == float(idx), val, out)
    route_ref[...] = out


def _proj_route(o, xs, mod, wo, ln_g, ln_b, wr_hi, wr_lo, br, tiles_per_sample, ctx_tiles, ctx_row,
                skip_ctx):
    rows, d = xs.shape
    n_tiles = o.shape[0] // TM
    q_tiles = tiles_per_sample - ctx_tiles
    if skip_ctx:
        tile_of = lambda i: (i // q_tiles) * tiles_per_sample + ctx_tiles + i % q_tiles
    else:
        tile_of = lambda i: i
    row_map = _mod_row_map(tiles_per_sample, ctx_tiles, ctx_row)
    const = lambda i: (0, 0)
    return pl.pallas_call(
        functools.partial(_proj_route_kernel, d=d),
        grid=(n_tiles,),
        in_specs=[
            pl.BlockSpec((TM, d), lambda i: (i, 0)),
            pl.BlockSpec((TM, d), lambda i: (tile_of(i), 0)),
            pl.BlockSpec((None, 1, mod.shape[-1]), lambda i: row_map(tile_of(i))),
            pl.BlockSpec((d, d), const),
            pl.BlockSpec((1, d), const),
            pl.BlockSpec((1, d), const),
            pl.BlockSpec((d, LANES), const),
            pl.BlockSpec((d, LANES), const),
            pl.BlockSpec((1, LANES), const),
        ],
        out_specs=[
            pl.BlockSpec((TM, d), lambda i: (i, 0)),
            pl.BlockSpec((TM, d), lambda i: (i, 0)),
            pl.BlockSpec((TM, LANES), lambda i: (i, 0)),
            pl.BlockSpec((None, 1, LANES), lambda i: (i, 0, 0)),
        ],
        out_shape=[
            jax.ShapeDtypeStruct((n_tiles * TM, d), F32),
            jax.ShapeDtypeStruct((n_tiles * TM, d), F32),
            jax.ShapeDtypeStruct((n_tiles * TM, LANES), F32),
            jax.ShapeDtypeStruct((n_tiles, 1, LANES), F32),
        ],
        compiler_params=_cparams(("arbitrary",)),
        name="proj_route",
    )(o, xs, mod, wo, ln_g, ln_b, wr_hi, wr_lo, br)


def _dispatch_plan(counts, max_tiles, expert_base):
    n = counts[:, 0, :N_EXPERTS].astype(jnp.int32)
    units = (n + CHUNK - 1) // CHUNK
    local_off = jnp.cumsum(units, axis=1) - units
    total = jnp.sum(units, axis=0)
    tiles_e = (total + UNITS_PER_TILE - 1) // UNITS_PER_TILE
    tile_end = jnp.cumsum(tiles_e)
    region_off = (tile_end - tiles_e) * UNITS_PER_TILE
    base = region_off[None, :] + jnp.cumsum(units, axis=0) - units
    n_tiles = tile_end[-1:]
    tile_ids = jnp.arange(max_tiles, dtype=jnp.int32)
    tile_expert = jnp.sum((tile_end[None, :] <= tile_ids[:, None]).astype(jnp.int32), axis=1)
    tile_expert = jnp.minimum(tile_expert, N_EXPERTS - 1)
    tail_units = tiles_e * UNITS_PER_TILE - total
    tail_off = region_off + total
    local_off_rows = jnp.zeros((n.shape[0], 1, LANES), F32)
    local_off_rows = local_off_rows.at[:, 0, :N_EXPERTS].set((local_off * CHUNK).astype(F32))
    unit_ids = jnp.arange(LOCAL_UNITS, dtype=jnp.int32)
    local_end = local_off + units
    expert_of = jnp.sum((local_end[:, None, :] <= unit_ids[None, :, None]).astype(jnp.int32), axis=-1)
    onehot = (expert_of[:, :, None] == jnp.arange(N_EXPERTS, dtype=jnp.int32)).astype(jnp.int32)
    global_unit = jnp.sum(onehot * (base - local_off)[:, None, :], axis=-1) + unit_ids[None, :]
    i32 = lambda a: a.reshape(-1).astype(jnp.int32)
    return dict(tile_units=i32(jnp.sum(units, axis=1)), global_unit=i32(global_unit),
                tail_units=i32(tail_units), tail_off=i32(tail_off), n_tiles=i32(n_tiles),
                tile_expert=i32(tile_expert) + expert_base, local_off_rows=local_off_rows)


def _for_each_unit(tile_units_ref, global_unit_ref, tile, fn):
    def body(j, carry):
        g = global_unit_ref[tile * LOCAL_UNITS + j]
        fn(pl.multiple_of(j * CHUNK, CHUNK), pl.multiple_of(g * CHUNK, CHUNK))
        return carry
    lax.fori_loop(0, tile_units_ref[tile], body, 0)


def _local_positions(route, local_off_rows):
    lane = lax.broadcasted_iota(jnp.int32, route.shape, 1).astype(F32)
    pos = []
    for k in range(2):
        onehot = (lane == route[:, ROUTE_E + k:ROUTE_E + k + 1]).astype(F32)
        off = jnp.sum(onehot * local_off_rows, axis=-1, keepdims=True)
        pos.append(off + route[:, ROUTE_RANK + k:ROUTE_RANK + k + 1])
    return pos


def _selection(pos):
    slot = lax.broadcasted_iota(jnp.int32, (pos.shape[0], LOCAL_ROWS), 1).astype(F32)
    return slot == pos


def _split3(w):
    hi = w.astype(BF16)
    r1 = w - hi.astype(F32)
    mid = r1.astype(BF16)
    lo = (r1 - mid.astype(F32)).astype(BF16)
    return hi, mid, lo


def _dispatch_kernel(tile_units_ref, global_unit_ref, tailn_ref, tailoff_ref, nt_ref,
                     h2_ref, route_ref, loff_ref, xs_hbm, buf, zbuf, sems, zsem, *, d, n_tok_tiles, max_tiles):
    t = pl.program_id(0)
    slot = t % 2
    route = route_ref[...]
    pos1, pos2 = _local_positions(route, loff_ref[...])
    sel1 = _selection(pos1)
    sel2 = _selection(pos2)
    contract0 = (((0,), (0,)), ((), ()))
    sel = (sel1 | sel2).astype(BF16)
    buf[slot, :, 0:d] = lax.dot_general(sel, h2_ref[...].astype(BF16), contract0, preferred_element_type=F32)

    lane = lax.broadcasted_iota(jnp.int32, route.shape, 1)
    gate_rows = jnp.zeros((LOCAL_ROWS, LANES), F32)
    for k, selk in ((0, sel1), (1, sel2)):
        pieces = _split3(route[:, ROUTE_W + k:ROUTE_W + k + 1])
        wp = jnp.zeros(route.shape, F32)
        for j, piece in enumerate(pieces):
            wp = jnp.where(lane == j, piece.astype(F32), wp)
        gate_rows += lax.dot_general(selk.astype(BF16), wp.astype(BF16), contract0,
                                     preferred_element_type=F32)
    buf[slot, :, d:] = gate_rows

    def push(s, wait):
        def fn(local_row, global_row):
            cp = pltpu.make_async_copy(buf.at[s, pl.ds(local_row, CHUNK), :],
                                       xs_hbm.at[pl.ds(global_row, CHUNK), :], sems.at[s])
            cp.wait() if wait else cp.start()
        return fn

    _for_each_unit(tile_units_ref, global_unit_ref, t, push(slot, False))

    @pl.when(t >= 1)
    def _():
        _for_each_unit(tile_units_ref, global_unit_ref, t - 1, push(1 - slot, True))

    @pl.when(t == n_tok_tiles - 1)
    def _():
        _for_each_unit(tile_units_ref, global_unit_ref, t, push(slot, True))
        zbuf[...] = jnp.zeros_like(zbuf)

        def tail_copy(e, bit):
            n = tailn_ref[e]
            done = (n >> (bit + 1)) << (bit + 1)
            row = pl.multiple_of((tailoff_ref[e] + done) * CHUNK, CHUNK)
            rows = (1 << bit) * CHUNK
            return pltpu.make_async_copy(zbuf.at[pl.ds(0, rows), :], xs_hbm.at[pl.ds(row, rows), :], zsem)

        def tile_copy(i):
            row = pl.multiple_of(i * TE, TE)
            return pltpu.make_async_copy(zbuf, xs_hbm.at[pl.ds(row, TE), :], zsem)

        for wait in (False, True):
            def tails(e, carry, wait=wait):
                for bit in reversed(range(TAIL_BITS)):
                    @pl.when(((tailn_ref[e] >> bit) & 1) == 1)
                    def _():
                        cp = tail_copy(e, bit)
                        cp.wait() if wait else cp.start()
                return carry
            lax.fori_loop(0, N_EXPERTS, tails, 0)

            def unused(i, carry, wait=wait):
                cp = tile_copy(i)
                cp.wait() if wait else cp.start()
                return carry
            lax.fori_loop(nt_ref[0], max_tiles, unused, 0)


def _dispatch(h2, route, plan, max_tiles):
    rows, d = h2.shape
    n_tok_tiles = rows // TM
    width = d + LANES
    return pl.pallas_call(
        functools.partial(_dispatch_kernel, d=d, n_tok_tiles=n_tok_tiles, max_tiles=max_tiles),
        grid_spec=pltpu.PrefetchScalarGridSpec(
            num_scalar_prefetch=5,
            grid=(n_tok_tiles,),
            in_specs=[
                pl.BlockSpec((TM, d), lambda t, *_: (t, 0)),
                pl.BlockSpec((TM, LANES), lambda t, *_: (t, 0)),
                pl.BlockSpec((None, 1, LANES), lambda t, *_: (t, 0, 0)),
            ],
            out_specs=pl.BlockSpec(memory_space=pl.ANY),
            scratch_shapes=[
                pltpu.VMEM((2, LOCAL_ROWS, width), F32),
                pltpu.VMEM((TE, width), F32),
                pltpu.SemaphoreType.DMA((2,)),
                pltpu.SemaphoreType.DMA(()),
            ],
        ),
        out_shape=jax.ShapeDtypeStruct((max_tiles * TE, width), F32),
        compiler_params=_cparams(("arbitrary",)),
        name="dispatch",
    )(plan["tile_units"], plan["global_unit"], plan["tail_units"], plan["tail_off"], plan["n_tiles"],
      h2, route, plan["local_off_rows"])


def _expert_kernel(te_ref, nt_ref, xs_ref, wg_ref, wu_ref, wd_ref, y_ref, wgu_b, wd_b, *, d, hidden):
    t = pl.program_id(0)
    nt = nt_ref[0]

    @pl.when(t < nt)
    def _():
        prev = te_ref[jnp.maximum(t - 1, 0)]

        @pl.when((t == 0) | (te_ref[t] != prev))
        def _():
            wgu_b[:, 0:hidden] = wg_ref[...].astype(BF16)
            wgu_b[:, hidden:2 * hidden] = wu_ref[...].astype(BF16)
            wd_b[...] = wd_ref[...].astype(BF16)

        x = xs_ref[:, 0:d].astype(BF16)
        gate = xs_ref[:, d:d + 1] + xs_ref[:, d + 1:d + 2] + xs_ref[:, d + 2:d + 3]
        au = jnp.dot(x, wgu_b[...], preferred_element_type=F32)
        a = au[:, 0:hidden]
        u = au[:, hidden:2 * hidden]
        act = (a * jax.nn.sigmoid(a) * u * gate).astype(BF16)
        y_ref[...] = jnp.dot(act, wd_b[...], preferred_element_type=F32)

    @pl.when(t >= nt)
    def _():
        y_ref[...] = jnp.zeros_like(y_ref)


def _experts(xs, plan, w_gate, w_up, w_down, max_tiles):
    width = xs.shape[1]
    d = width - LANES
    hidden = w_gate.shape[-1]
    last = lambda t, te, nt: jnp.minimum(t, nt[0] - 1)
    return pl.pallas_call(
        functools.partial(_expert_kernel, d=d, hidden=hidden),
        grid_spec=pltpu.PrefetchScalarGridSpec(
            num_scalar_prefetch=2,
            grid=(max_tiles,),
            in_specs=[
                pl.BlockSpec((TE, width), lambda t, te, nt: (last(t, te, nt), 0)),
                pl.BlockSpec((None, d, hidden), lambda t, te, nt: (te[last(t, te, nt)], 0, 0)),
                pl.BlockSpec((None, d, hidden), lambda t, te, nt: (te[last(t, te, nt)], 0, 0)),
                pl.BlockSpec((None, hidden, d), lambda t, te, nt: (te[last(t, te, nt)], 0, 0)),
            ],
            out_specs=pl.BlockSpec((TE, d), lambda t, te, nt: (t, 0)),
            scratch_shapes=[
                pltpu.VMEM((d, 2 * hidden), BF16),
                pltpu.VMEM((hidden, d), BF16),
            ],
        ),
        out_shape=jax.ShapeDtypeStruct((max_tiles * TE, d), F32),
        compiler_params=_cparams(("arbitrary",)),
        name="experts",
    )(plan["tile_expert"], plan["n_tiles"], xs, w_gate, w_up, w_down)


def _combine_kernel(tile_units_ref, global_unit_ref, y_hbm, x1_ref, route_ref, loff_ref, mod_ref, lng_ref,
                    lnb_ref, o_ref, ybuf, sems, *, d, n_tok_tiles):
    t = pl.program_id(0)
    slot = t % 2

    def pull(s, wait):
        def fn(local_row, global_row):
            cp = pltpu.make_async_copy(y_hbm.at[pl.ds(global_row, CHUNK), :],
                                       ybuf.at[s, pl.ds(local_row, CHUNK), :], sems.at[s])
            cp.wait() if wait else cp.start()
        return fn

    @pl.when(t == 0)
    def _():
        ybuf[...] = jnp.zeros_like(ybuf)
        _for_each_unit(tile_units_ref, global_unit_ref, 0, pull(0, False))

    @pl.when(t + 1 < n_tok_tiles)
    def _():
        _for_each_unit(tile_units_ref, global_unit_ref, t + 1, pull(1 - slot, False))

    _for_each_unit(tile_units_ref, global_unit_ref, t, pull(slot, True))

    pos1, pos2 = _local_positions(route_ref[...], loff_ref[...])
    sel = (_selection(pos1) | _selection(pos2)).astype(BF16)
    y = ybuf[slot]
    y_hi = y.astype(BF16)
    y_lo = (y - y_hi.astype(F32)).astype(BF16)
    fx = jnp.dot(sel, y_hi, preferred_element_type=F32) + jnp.dot(sel, y_lo, preferred_element_type=F32)
    gate_f = mod_ref[:, 5 * d:6 * d]
    o_ref[...] = _layer_norm(DEEPNORM_ALPHA * x1_ref[...] + gate_f * fx, lng_ref[...], lnb_ref[...])


def _combine(ys, plan, x1, route, mod, ln_g, ln_b, mod_tile_of):
    rows, d = x1.shape
    n_tok_tiles = rows // TM
    const = lambda t, *_: (0, 0)
    return pl.pallas_call(
        functools.partial(_combine_kernel, d=d, n_tok_tiles=n_tok_tiles),
        grid_spec=pltpu.PrefetchScalarGridSpec(
            num_scalar_prefetch=2,
            grid=(n_tok_tiles,),
            in_specs=[
                pl.BlockSpec(memory_space=pl.ANY),
                pl.BlockSpec((TM, d), lambda t, *_: (t, 0)),
                pl.BlockSpec((TM, LANES), lambda t, *_: (t, 0)),
                pl.BlockSpec((None, 1, LANES), lambda t, *_: (t, 0, 0)),
                pl.BlockSpec((None, 1, mod.shape[-1]), lambda t, *_: mod_tile_of(t)),
                pl.BlockSpec((1, d), const),
                pl.BlockSpec((1, d), const),
            ],
            out_specs=pl.BlockSpec((TM, d), lambda t, *_: (t, 0)),
            scratch_shapes=[
                pltpu.VMEM((2, LOCAL_ROWS, d), F32),
                pltpu.SemaphoreType.DMA((2,)),
            ],
        ),
        out_shape=jax.ShapeDtypeStruct((rows, d), F32),
        compiler_params=_cparams(("arbitrary",)),
        name="combine",
    )(plan["tile_units"], plan["global_unit"], ys, x1, route, plan["local_off_rows"], mod, ln_g, ln_b)


def kernel(x, c, ctx, c_ctx, w_mod, b_mod, ln_mix_g, ln_mix_b, ln_ffn_g, ln_ffn_b, diff_w_qkv, diff_w_o, diff_lambda_q1, diff_lambda_k1, diff_lambda_q2, diff_lambda_k2, diff_subln_g, gqa_w_qkv, gqa_w_o, gqa_q_norm_g, gqa_k_norm_g, moe_w_group, moe_b_group, moe_w_router, moe_b_router, moe_w_gate, moe_w_up, moe_w_down):
    batch, n, d = x.shape
    n_ctx = ctx.shape[1]
    t = n_ctx + n
    assert n % TM == 0 and n_ctx % TM == 0 and n % GRID_W == 0 and d % LANES == 0
    assert w_mod.shape[0] == DEPTH
    tiles_per_sample = t // TM
    ctx_tiles = n_ctx // TM
    q_tiles = tiles_per_sample - ctx_tiles

    pad = (-(batch + 1)) % SUBLANES
    cond = jnp.concatenate([c, c_ctx[None, :], jnp.zeros((pad, d), F32)], axis=0)
    ctx_row = batch
    mod_all = _modulation(cond, w_mod, b_mod)
    mod_all = mod_all.reshape(DEPTH, cond.shape[0], 1, N_MOD * d)

    xs = jnp.concatenate([ctx, x], axis=1).reshape(batch * t, d)

    for i in range(DEPTH):
        last = i == DEPTH - 1
        mod = mod_all[i]
        j = i // 2
        lng_m, lnb_m = ln_mix_g[i][None, :], ln_mix_b[i][None, :]
        lng_f, lnb_f = ln_ffn_g[i][None, :], ln_ffn_b[i][None, :]
        if i % 2 == 0:
            lambda_init = 0.8 - 0.6 * math.exp(-0.3 * i)
            tab = _rope_tables(n_ctx, n, DIFF_HEAD_DIM)
            q, k, v = _qkv_proj("diff", xs, mod, diff_w_qkv[j].astype(BF16), tab, None,
                                tiles_per_sample, ctx_tiles, ctx_row)
            lam_vecs = jnp.stack([diff_lambda_q1[j], diff_lambda_k1[j], diff_lambda_q2[j], diff_lambda_k2[j]])
            o = _diff_attention(q, k, v, lam_vecs.astype(F32), diff_subln_g[j][None, :], batch, t, n_ctx,
                                lambda_init)
            if last:
                o = o.reshape(batch, t, d)[:, n_ctx:].reshape(batch * n, d)
            wo = diff_w_o[j]
        else:
            tab = _rope_tables(n_ctx, n, GQA_HEAD_DIM)
            q, k, v = _qkv_proj("gqa", xs, mod, gqa_w_qkv[j].astype(BF16), tab,
                                (gqa_q_norm_g[j][None, :], gqa_k_norm_g[j][None, :]),
                                tiles_per_sample, ctx_tiles, ctx_row)
            o = _gqa_attention(q, k, v, batch, t, n_ctx)
            if not last:
                raise NotImplementedError("grouped-query layer with context outputs")
            wo = gqa_w_o[j]

        w_r = jnp.zeros((d, LANES), F32)
        w_r = w_r.at[:, :N_EXPERTS].set(moe_w_router[i]).at[:, GROUP_LANE0:GROUP_LANE0 + N_GROUPS].set(moe_w_group[i])
        b_r = jnp.zeros((1, LANES), F32)
        b_r = b_r.at[0, :N_EXPERTS].set(moe_b_router[i]).at[0, GROUP_LANE0:GROUP_LANE0 + N_GROUPS].set(moe_b_group[i])
        wr_hi = w_r.astype(BF16)
        wr_lo = (w_r - wr_hi.astype(F32)).astype(BF16)

        x1, h2, route, counts = _proj_route(o, xs, mod, wo.astype(BF16), lng_m, lnb_m, wr_hi, wr_lo, b_r,
                                            tiles_per_sample, ctx_tiles, ctx_row, skip_ctx=last)
        n_tok_tiles = x1.shape[0] // TM
        max_tiles = (2 * n_tok_tiles * TM + n_tok_tiles * N_EXPERTS * (CHUNK - 1)) // TE + N_EXPERTS
        plan = _dispatch_plan(counts, max_tiles, expert_base=i * N_EXPERTS)
        xsorted = _dispatch(h2, route, plan, max_tiles)
        hidden = moe_w_gate.shape[-1]
        w_gate = moe_w_gate.reshape(DEPTH * N_EXPERTS, d, hidden)
        w_up = moe_w_up.reshape(DEPTH * N_EXPERTS, d, hidden)
        w_down = moe_w_down.reshape(DEPTH * N_EXPERTS, hidden, d)
        ys = _experts(xsorted, plan, w_gate, w_up, w_down, max_tiles)

        row_map = _mod_row_map(tiles_per_sample, ctx_tiles, ctx_row)
        if last:
            mod_tile_of = lambda tt: row_map((tt // q_tiles) * tiles_per_sample + ctx_tiles + tt % q_tiles)
        else:
            mod_tile_of = row_map
        xs = _combine(ys, plan, x1, route, mod, lng_f, lnb_f, mod_tile_of)

    return xs.reshape(batch, n, d)
```

```python
import functools
import math

import jax
import jax.numpy as jnp
from jax import lax
from jax.experimental import pallas as pl
from jax.experimental.pallas import tpu as pltpu

F32 = jnp.float32
BF16 = jnp.bfloat16

GRID_W = 64
DIFF_HEAD_DIM = 64
GQA_HEAD_DIM = 128
GQA_KV_HEADS = 2
ROPE_THETA = 10000.0
N_GROUPS = 4
EXPERTS_PER_GROUP = 8
N_EXPERTS = N_GROUPS * EXPERTS_PER_GROUP
N_MOD = 6
LN_EPS = 1e-5
RMS_EPS = 1e-6
DEPTH = 2
DEEPNORM_ALPHA = (2 * DEPTH) ** 0.25
LOG2E = 1.4426950408889634

LANES = 128
SUBLANES = 8
TM = 256
TE = 512
ATTN_KC = 1280
ATTN_Q_TILES = 8
ATTN_SUB_TILES = 2
VMEM_LIMIT = 48 * 1024 * 1024

GROUP_LANE0 = N_EXPERTS

CHUNK = SUBLANES
UNITS_PER_TILE = TE // CHUNK
TAIL_BITS = (UNITS_PER_TILE - 1).bit_length()
LOCAL_ROWS = 2 * TM + N_EXPERTS * CHUNK
LOCAL_UNITS = LOCAL_ROWS // CHUNK
ROUTE_E, ROUTE_W, ROUTE_RANK = 0, 2, 4


def _cparams(sem):
    return pltpu.CompilerParams(dimension_semantics=sem, vmem_limit_bytes=VMEM_LIMIT)


def _mod_kernel(c_ref, w_ref, b_ref, o_ref):
    c = c_ref[...]
    s = c * jax.nn.sigmoid(c)
    w = w_ref[...]
    sh = s.astype(BF16)
    sl = (s - sh.astype(F32)).astype(BF16)
    wh = w.astype(BF16)
    wl = (w - wh.astype(F32)).astype(BF16)
    acc = jnp.dot(sh, wh, preferred_element_type=F32)
    acc += jnp.dot(sl, wh, preferred_element_type=F32)
    acc += jnp.dot(sh, wl, preferred_element_type=F32)
    o_ref[...] = acc + b_ref[...]


def _modulation(cond, w_mod, b_mod):
    depth, d, width = w_mod.shape
    r = cond.shape[0]
    tn = 512
    return pl.pallas_call(
        _mod_kernel,
        grid=(depth, width // tn),
        in_specs=[
            pl.BlockSpec((r, d), lambda i, j: (0, 0)),
            pl.BlockSpec((None, d, tn), lambda i, j: (i, 0, j)),
            pl.BlockSpec((None, 1, tn), lambda i, j: (i, 0, j)),
        ],
        out_specs=pl.BlockSpec((None, r, tn), lambda i, j: (i, 0, j)),
        out_shape=jax.ShapeDtypeStruct((depth, r, width), F32),
        compiler_params=_cparams(("arbitrary", "arbitrary")),
        name="modulation",
    )(cond, w_mod, b_mod.reshape(depth, 1, width))


def _rope_tables(n_ctx, n, head_dim):
    rows = n // GRID_W
    row = jnp.broadcast_to(jnp.arange(rows, dtype=F32)[:, None], (rows, GRID_W)).reshape(-1)
    col = jnp.broadcast_to(jnp.arange(GRID_W, dtype=F32)[None, :], (rows, GRID_W)).reshape(-1)
    axis_dim = head_dim // 2
    inv_freq = ROPE_THETA ** (-jnp.arange(0, axis_dim, 2, dtype=F32) / axis_dim)
    ang = jnp.stack([row, col], axis=-1)[:, :, None] * inv_freq
    cos, sin = jnp.cos(ang), jnp.sin(ang)
    zero = jnp.zeros_like(sin)
    c = jnp.concatenate([cos, cos], axis=-1).reshape(n, head_dim)
    s_first = jnp.concatenate([-sin, zero], axis=-1).reshape(n, head_dim)
    s_second = jnp.concatenate([zero, sin], axis=-1).reshape(n, head_dim)
    tab = jnp.stack([c, s_first, s_second])
    tab = jnp.tile(tab, (1, 1, LANES // head_dim))
    ident = jnp.stack([jnp.ones((n_ctx, LANES), F32), jnp.zeros((n_ctx, LANES), F32),
                       jnp.zeros((n_ctx, LANES), F32)])
    return jnp.concatenate([ident, tab], axis=1)


def _rope_chunk(x, c, s_first, s_second, quarter):
    return x * c + pltpu.roll(x, LANES - quarter, 1) * s_first + pltpu.roll(x, quarter, 1) * s_second


def _qkv_diff_kernel(x_ref, mod_ref, w_ref, tab_ref, q_ref, k_ref, v_ref, *, d, qscale):
    x = x_ref[...]
    shift = mod_ref[:, 0:d]
    scale = mod_ref[:, d:2 * d]
    h = (x * (1.0 + scale) + shift).astype(BF16)
    qkv = jnp.dot(h, w_ref[...], preferred_element_type=F32)
    c, s_first, s_second = tab_ref[0], tab_ref[1], tab_ref[2]
    quarter = DIFF_HEAD_DIM // 4
    for j in range(d // LANES):
        lo, hi = j * LANES, (j + 1) * LANES
        q = _rope_chunk(qkv[:, lo:hi], c, s_first, s_second, quarter)
        q_ref[:, lo:hi] = (q * qscale).astype(BF16)
        k = _rope_chunk(qkv[:, d + lo:d + hi], c, s_first, s_second, quarter)
        k_ref[:, lo:hi] = k.astype(BF16)
    v_ref[...] = qkv[:, 2 * d:].astype(BF16)


def _rms_head(x, g):
    return x * lax.rsqrt(jnp.mean(x * x, axis=-1, keepdims=True) + RMS_EPS) * g


def _qkv_gqa_kernel(x_ref, mod_ref, w_ref, tab_ref, qg_ref, kg_ref, q_ref, k_ref, v_ref, *, d, qscale):
    x = x_ref[...]
    shift = mod_ref[:, 0:d]
    scale = mod_ref[:, d:2 * d]
    h = (x * (1.0 + scale) + shift).astype(BF16)
    qkv = jnp.dot(h, w_ref[...], preferred_element_type=F32)
    c, s_first, s_second = tab_ref[0], tab_ref[1], tab_ref[2]
    quarter = GQA_HEAD_DIM // 4
    kv_w = GQA_KV_HEADS * GQA_HEAD_DIM
    for j in range(d // LANES):
        lo, hi = j * LANES, (j + 1) * LANES
        q = _rope_chunk(_rms_head(qkv[:, lo:hi], qg_ref[...]), c, s_first, s_second, quarter)
        q_ref[:, lo:hi] = (q * qscale).astype(BF16)
    for j in range(GQA_KV_HEADS):
        lo, hi = j * LANES, (j + 1) * LANES
        k = _rope_chunk(_rms_head(qkv[:, d + lo:d + hi], kg_ref[...]), c, s_first, s_second, quarter)
        k_ref[:, lo:hi] = k.astype(BF16)
    v_ref[...] = qkv[:, d + kv_w:].astype(BF16)


def _mod_row_map(tiles_per_sample, ctx_tiles, ctx_row):
    def index(i):
        b = i // tiles_per_sample
        j = i % tiles_per_sample
        return (jnp.where(j < ctx_tiles, ctx_row, b), 0, 0)
    return index


def _qkv_proj(kind, xs, mod, w, tab, norm_g, tiles_per_sample, ctx_tiles, ctx_row):
    rows, d = xs.shape
    n_tiles = rows // TM
    width = w.shape[1]
    row_map = _mod_row_map(tiles_per_sample, ctx_tiles, ctx_row)
    in_specs = [
        pl.BlockSpec((TM, d), lambda i: (i, 0)),
        pl.BlockSpec((None, 1, mod.shape[-1]), row_map),
        pl.BlockSpec((d, width), lambda i: (0, 0)),
        pl.BlockSpec((3, TM, LANES), lambda i: (0, i % tiles_per_sample, 0)),
    ]
    args = [xs, mod, w, tab]
    if kind == "diff":
        kern = functools.partial(_qkv_diff_kernel, d=d, qscale=DIFF_HEAD_DIM ** -0.5 * LOG2E)
        kw, vw = d, d
    else:
        kern = functools.partial(_qkv_gqa_kernel, d=d, qscale=GQA_HEAD_DIM ** -0.5 * LOG2E)
        kw = vw = GQA_KV_HEADS * GQA_HEAD_DIM
        in_specs += [pl.BlockSpec((1, LANES), lambda i: (0, 0))] * 2
        args += list(norm_g)
    return pl.pallas_call(
        kern,
        grid=(n_tiles,),
        in_specs=in_specs,
        out_specs=[
            pl.BlockSpec((TM, d), lambda i: (i, 0)),
            pl.BlockSpec((TM, kw), lambda i: (i, 0)),
            pl.BlockSpec((TM, vw), lambda i: (i, 0)),
        ],
        out_shape=[
            jax.ShapeDtypeStruct((rows, d), BF16),
            jax.ShapeDtypeStruct((rows, kw), BF16),
            jax.ShapeDtypeStruct((rows, vw), BF16),
        ],
        compiler_params=_cparams(("arbitrary",)),
        name="qkv_" + kind,
    )(*args)


def _flash(q, k_ref, va_ref, n_keys):
    m = None
    acc = None
    for lo in range(0, n_keys, ATTN_KC):
        hi = min(lo + ATTN_KC, n_keys)
        s = lax.dot_general(q, k_ref[lo:hi, :], (((1,), (1,)), ((), ())), preferred_element_type=F32)
        m_new = jnp.max(s, axis=-1, keepdims=True)
        if m is not None:
            m_new = jnp.maximum(m, m_new)
        p = jnp.exp2(s - m_new).astype(BF16)
        pv = jnp.dot(p, va_ref[lo:hi, :], preferred_element_type=F32)
        acc = pv if acc is None else jnp.exp2(m - m_new) * acc + pv
        m = m_new
    return acc[:, 0:LANES] / acc[:, LANES:LANES + 1]


def _fill_values(v_ref, va_ref):
    va_ref[:, 0:LANES] = v_ref[...]
    lane = lax.broadcasted_iota(jnp.int32, v_ref.shape, 1)
    va_ref[:, LANES:2 * LANES] = jnp.where(lane == 0, 1.0, 0.0).astype(va_ref.dtype)


def _diff_attn_kernel(lam_ref, g_ref, *refs, n_q, lambda_init):
    q_refs = refs[:n_q]
    k_ref, v_ref, o_ref, va_ref = refs[n_q:]

    @pl.when(pl.program_id(2) == 0)
    def _():
        _fill_values(v_ref, va_ref)

    lv = lam_ref[...]
    lam = (jnp.exp(jnp.sum(lv[0:1] * lv[1:2], axis=-1, keepdims=True))
           - jnp.exp(jnp.sum(lv[2:3] * lv[3:4], axis=-1, keepdims=True)) + lambda_init)
    per = min(ATTN_SUB_TILES, n_q)
    for sub in range(n_q // per):
        parts = [r[...] for r in q_refs[sub * per:(sub + 1) * per]]
        q = jnp.concatenate(parts, axis=0) if per > 1 else parts[0]
        rows = q.shape[0]
        lane = lax.broadcasted_iota(jnp.int32, q.shape, 1)
        zero = jnp.zeros_like(q)
        q12 = jnp.concatenate([jnp.where(lane < DIFF_HEAD_DIM, q, zero),
                               jnp.where(lane >= DIFF_HEAD_DIM, q, zero)], axis=0)
        o12 = _flash(q12, k_ref, va_ref, k_ref.shape[0])
        o = o12[0:rows] - lam * o12[rows:2 * rows]
        o = o * lax.rsqrt(jnp.mean(o * o, axis=-1, keepdims=True) + RMS_EPS) * g_ref[...]
        o_ref[sub * rows:(sub + 1) * rows, :] = (o * (1.0 - lambda_init)).astype(o_ref.dtype)


def _diff_attention(q, k, v, lam_vecs, subln_g, batch, t, n_ctx, lambda_init):
    rows, d = q.shape
    heads = d // LANES
    tiles = t // TM
    ctx_tiles = n_ctx // TM
    q_steps = (tiles - ctx_tiles) // ATTN_Q_TILES
    assert t % n_ctx == 0 and (tiles - ctx_tiles) % ATTN_Q_TILES == 0
    const = lambda b, h, i: (0, 0)

    def call(name, n_q, q_tile_of, kv_rows, kv_block, out_rows, out_block_rows, out_index, steps):
        return pl.pallas_call(
            functools.partial(_diff_attn_kernel, n_q=n_q, lambda_init=lambda_init),
            grid=(batch, heads, steps),
            in_specs=[pl.BlockSpec(lam_vecs.shape, const), pl.BlockSpec((1, LANES), const)]
            + [pl.BlockSpec((TM, LANES), functools.partial(q_tile_of, s=s)) for s in range(n_q)]
            + [pl.BlockSpec((kv_rows, LANES), kv_block)] * 2,
            out_specs=pl.BlockSpec((out_block_rows, LANES), out_index),
            out_shape=jax.ShapeDtypeStruct((out_rows, d), BF16),
            scratch_shapes=[pltpu.VMEM((kv_rows, 2 * LANES), BF16)],
            compiler_params=_cparams(("arbitrary", "arbitrary", "arbitrary")),
            name=name,
        )(lam_vecs, subln_g, *([q] * n_q), k, v)

    o_ctx = call("diff_attention_ctx", 1, lambda b, h, i, s: (b * tiles + i, h),
                 n_ctx, lambda b, h, i: (b * (t // n_ctx), h),
                 batch * n_ctx, TM, lambda b, h, i: (b * ctx_tiles + i, h), ctx_tiles)
    o_lat = call("diff_attention", ATTN_Q_TILES,
                 lambda b, h, i, s: (b * tiles + ctx_tiles + ATTN_Q_TILES * i + s, h),
                 t, lambda b, h, i: (b, h),
                 batch * (t - n_ctx), ATTN_Q_TILES * TM, lambda b, h, i: (b * q_steps + i, h), q_steps)
    return o_ctx, o_lat


def _gqa_attn_kernel(q_ref, k_ref, v_ref, o_ref, va_ref, *, group):
    @pl.when(pl.program_id(2) == 0)
    def _():
        _fill_values(v_ref, va_ref)

    rows = q_ref.shape[0]
    n_keys = k_ref.shape[0]
    q_all = jnp.concatenate([q_ref[:, g * LANES:(g + 1) * LANES] for g in range(group)], axis=0)
    o_all = _flash(q_all, k_ref, va_ref, n_keys)
    for g in range(group):
        o_ref[:, g * LANES:(g + 1) * LANES] = o_all[g * rows:(g + 1) * rows].astype(o_ref.dtype)


def _gqa_attention(q, k, v, batch, t, n_ctx):
    rows, d = q.shape
    group = d // GQA_HEAD_DIM // GQA_KV_HEADS
    tiles = t // TM
    ctx_tiles = n_ctx // TM
    q_tiles = tiles - ctx_tiles
    gw = group * LANES
    return pl.pallas_call(
        functools.partial(_gqa_attn_kernel, group=group),
        grid=(batch, GQA_KV_HEADS, q_tiles),
        in_specs=[
            pl.BlockSpec((TM, gw), lambda b, h, i: (b * tiles + ctx_tiles + i, h)),
            pl.BlockSpec((t, LANES), lambda b, h, i: (b, h)),
            pl.BlockSpec((t, LANES), lambda b, h, i: (b, h)),
        ],
        out_specs=pl.BlockSpec((TM, gw), lambda b, h, i: (b * q_tiles + i, h)),
        out_shape=jax.ShapeDtypeStruct((batch * q_tiles * TM, d), BF16),
        scratch_shapes=[pltpu.VMEM((t, 2 * LANES), BF16)],
        compiler_params=_cparams(("arbitrary", "arbitrary", "arbitrary")),
        name="gqa_attention",
    )(q, k, v)


def _layer_norm(y, g, b):
    mu = jnp.mean(y, axis=-1, keepdims=True)
    yc = y - mu
    var = jnp.mean(yc * yc, axis=-1, keepdims=True)
    return yc * lax.rsqrt(var + LN_EPS) * g + b


def _proj_route_kernel(*refs, d, tiles_per_sample, ctx_tiles, with_ctx):
    if with_ctx:
        octx_ref, olat_ref = refs[:2]
        tile_in_sample = jnp.full(olat_ref.shape, pl.program_id(0) % tiles_per_sample, jnp.int32)
        o = jnp.where(tile_in_sample < ctx_tiles, octx_ref[...], olat_ref[...])
        refs = refs[2:]
    else:
        o = refs[0][...]
        refs = refs[1:]
    (x_ref, mod_ref, wo_ref, lng_ref, lnb_ref, wrh_ref, wrl_ref, br_ref,
     x1_ref, h2_ref, route_ref, cnt_ref) = refs
    gate_m = mod_ref[:, 2 * d:3 * d]
    shift_f = mod_ref[:, 3 * d:4 * d]
    scale_f = mod_ref[:, 4 * d:5 * d]
    ox = jnp.dot(o, wo_ref[...], preferred_element_type=F32)
    x1 = _layer_norm(DEEPNORM_ALPHA * x_ref[...] + gate_m * ox, lng_ref[...], lnb_ref[...])
    x1_ref[...] = x1
    h2 = x1 * (1.0 + scale_f) + shift_f
    h2_ref[...] = h2

    hh = h2.astype(BF16)
    hl = (h2 - hh.astype(F32)).astype(BF16)
    logits = jnp.dot(hh, wrh_ref[...], preferred_element_type=F32)
    logits += jnp.dot(hl, wrh_ref[...], preferred_element_type=F32)
    logits += jnp.dot(hh, wrl_ref[...], preferred_element_type=F32)
    logits += br_ref[...]

    lane = lax.broadcasted_iota(jnp.int32, logits.shape, 1).astype(F32)
    neg = jnp.full_like(logits, -jnp.inf)
    big = jnp.full_like(logits, 1e9)
    is_group = (lane >= GROUP_LANE0) & (lane < GROUP_LANE0 + N_GROUPS)
    lg = jnp.where(is_group, logits, neg)
    g_max = jnp.max(lg, axis=-1, keepdims=True)
    g_idx = jnp.min(jnp.where(lg == g_max, lane - GROUP_LANE0, big), axis=-1, keepdims=True)
    g_top = 1.0 / jnp.sum(jnp.exp(lg - g_max), axis=-1, keepdims=True)

    lane_group = jnp.floor(lane * (1.0 / EXPERTS_PER_GROUP))
    in_group = (lane < N_EXPERTS) & (lane_group == g_idx)
    le = jnp.where(in_group, logits, neg)
    m1 = jnp.max(le, axis=-1, keepdims=True)
    i1 = jnp.min(jnp.where(le == m1, lane, big), axis=-1, keepdims=True)
    le2 = jnp.where(lane == i1, neg, le)
    m2 = jnp.max(le2, axis=-1, keepdims=True)
    i2 = jnp.min(jnp.where(le2 == m2, lane, big), axis=-1, keepdims=True)
    r = jnp.exp(m2 - m1)
    w1 = g_top / (1.0 + r)
    w2 = g_top * r / (1.0 + r)

    a1 = (lane == i1).astype(F32)
    a2 = (lane == i2).astype(F32)
    both = (a1 + a2).astype(BF16)
    tm = logits.shape[0]
    rr = lax.broadcasted_iota(jnp.int32, (tm, tm), 0)
    cc = lax.broadcasted_iota(jnp.int32, (tm, tm), 1)
    strict_lower = (rr > cc).astype(BF16)
    before = jnp.dot(strict_lower, both, preferred_element_type=F32)
    rank1 = jnp.sum(a1 * before, axis=-1, keepdims=True)
    rank2 = jnp.sum(a2 * before, axis=-1, keepdims=True)
    cnt_ref[...] = jnp.sum(a1 + a2, axis=0, keepdims=True)

    out = jnp.zeros_like(logits)
    for idx, val in ((ROUTE_E, i1), (ROUTE_E + 1, i2), (ROUTE_W, w1), (ROUTE_W + 1, w2),
                     (ROUTE_RANK, rank1), (ROUTE_RANK + 1, rank2)):
        out = jnp.where(lane == float(idx), val, out)
    route_ref[...] = out


def _proj_route(o_parts, xs, mod, wo, ln_g, ln_b, wr_hi, wr_lo, br, tiles_per_sample, ctx_tiles, ctx_row):
    rows, d = xs.shape
    q_tiles = tiles_per_sample - ctx_tiles
    with_ctx = len(o_parts) == 2
    if with_ctx:
        n_tiles = rows // TM
        tile_of = lambda i: i
        o_specs = [
            pl.BlockSpec((TM, d), lambda i: ((i // tiles_per_sample) * ctx_tiles
                                             + jnp.minimum(i % tiles_per_sample, ctx_tiles - 1), 0)),
            pl.BlockSpec((TM, d), lambda i: ((i // tiles_per_sample) * q_tiles
                                             + jnp.maximum(i % tiles_per_sample - ctx_tiles, 0), 0)),
        ]
    else:
        n_tiles = o_parts[0].shape[0] // TM
        tile_of = lambda i: (i // q_tiles) * tiles_per_sample + ctx_tiles + i % q_tiles
        o_specs = [pl.BlockSpec((TM, d), lambda i: (i, 0))]
    row_map = _mod_row_map(tiles_per_sample, ctx_tiles, ctx_row)
    const = lambda i: (0, 0)
    return pl.pallas_call(
        functools.partial(_proj_route_kernel, d=d, tiles_per_sample=tiles_per_sample, ctx_tiles=ctx_tiles,
                          with_ctx=with_ctx),
        grid=(n_tiles,),
        in_specs=o_specs + [
            pl.BlockSpec((TM, d), lambda i: (tile_of(i), 0)),
            pl.BlockSpec((None, 1, mod.shape[-1]), lambda i: row_map(tile_of(i))),
            pl.BlockSpec((d, d), const),
            pl.BlockSpec((1, d), const),
            pl.BlockSpec((1, d), const),
            pl.BlockSpec((d, LANES), const),
            pl.BlockSpec((d, LANES), const),
            pl.BlockSpec((1, LANES), const),
        ],
        out_specs=[
            pl.BlockSpec((TM, d), lambda i: (i, 0)),
            pl.BlockSpec((TM, d), lambda i: (i, 0)),
            pl.BlockSpec((TM, LANES), lambda i: (i, 0)),
            pl.BlockSpec((None, 1, LANES), lambda i: (i, 0, 0)),
        ],
        out_shape=[
            jax.ShapeDtypeStruct((n_tiles * TM, d), F32),
            jax.ShapeDtypeStruct((n_tiles * TM, d), F32),
            jax.ShapeDtypeStruct((n_tiles * TM, LANES), F32),
            jax.ShapeDtypeStruct((n_tiles, 1, LANES), F32),
        ],
        compiler_params=_cparams(("arbitrary",)),
        name="proj_route",
    )(*o_parts, xs, mod, wo, ln_g, ln_b, wr_hi, wr_lo, br)


def _dispatch_plan(counts, max_tiles, expert_base):
    n = counts[:, 0, :N_EXPERTS].astype(jnp.int32)
    units = (n + CHUNK - 1) // CHUNK
    local_off = jnp.cumsum(units, axis=1) - units
    total = jnp.sum(units, axis=0)
    tiles_e = (total + UNITS_PER_TILE - 1) // UNITS_PER_TILE
    tile_end = jnp.cumsum(tiles_e)
    region_off = (tile_end - tiles_e) * UNITS_PER_TILE
    base = region_off[None, :] + jnp.cumsum(units, axis=0) - units
    n_tiles = tile_end[-1:]
    tile_ids = jnp.arange(max_tiles, dtype=jnp.int32)
    tile_expert = jnp.sum((tile_end[None, :] <= tile_ids[:, None]).astype(jnp.int32), axis=1)
    tile_expert = jnp.minimum(tile_expert, N_EXPERTS - 1)
    tail_units = tiles_e * UNITS_PER_TILE - total
    tail_off = region_off + total
    local_off_rows = jnp.zeros((n.shape[0], 1, LANES), F32)
    local_off_rows = local_off_rows.at[:, 0, :N_EXPERTS].set((local_off * CHUNK).astype(F32))
    unit_ids = jnp.arange(LOCAL_UNITS, dtype=jnp.int32)
    local_end = local_off + units
    expert_of = jnp.sum((local_end[:, None, :] <= unit_ids[None, :, None]).astype(jnp.int32), axis=-1)
    onehot = (expert_of[:, :, None] == jnp.arange(N_EXPERTS, dtype=jnp.int32)).astype(jnp.int32)
    global_unit = jnp.sum(onehot * (base - local_off)[:, None, :], axis=-1) + unit_ids[None, :]
    i32 = lambda a: a.reshape(-1).astype(jnp.int32)
    return dict(tile_units=i32(jnp.sum(units, axis=1)), global_unit=i32(global_unit),
                tail_units=i32(tail_units), tail_off=i32(tail_off), n_tiles=i32(n_tiles),
                tile_expert=i32(tile_expert) + expert_base, local_off_rows=local_off_rows)


def _for_each_unit(tile_units_ref, global_unit_ref, tile, fn):
    def body(j, carry):
        g = global_unit_ref[tile * LOCAL_UNITS + j]
        fn(pl.multiple_of(j * CHUNK, CHUNK), pl.multiple_of(g * CHUNK, CHUNK))
        return carry
    lax.fori_loop(0, tile_units_ref[tile], body, 0)


def _local_positions(route, local_off_rows):
    lane = lax.broadcasted_iota(jnp.int32, route.shape, 1).astype(F32)
    pos = []
    for k in range(2):
        onehot = (lane == route[:, ROUTE_E + k:ROUTE_E + k + 1]).astype(F32)
        off = jnp.sum(onehot * local_off_rows, axis=-1, keepdims=True)
        pos.append(off + route[:, ROUTE_RANK + k:ROUTE_RANK + k + 1])
    return pos


def _selection(pos):
    slot = lax.broadcasted_iota(jnp.int32, (pos.shape[0], LOCAL_ROWS), 1).astype(F32)
    return slot == pos


def _split3(w):
    hi = w.astype(BF16)
    r1 = w - hi.astype(F32)
    mid = r1.astype(BF16)
    lo = (r1 - mid.astype(F32)).astype(BF16)
    return hi, mid, lo


def _dispatch_kernel(tile_units_ref, global_unit_ref, tailn_ref, tailoff_ref, nt_ref,
                     h2_ref, route_ref, loff_ref, xs_hbm, buf, zbuf, sems, zsem, *, d, n_tok_tiles, max_tiles):
    t = pl.program_id(0)
    slot = t % 2
    route = route_ref[...]
    pos1, pos2 = _local_positions(route, loff_ref[...])
    sel1 = _selection(pos1)
    sel2 = _selection(pos2)
    contract0 = (((0,), (0,)), ((), ()))
    sel = (sel1 | sel2).astype(BF16)
    buf[slot, :, 0:d] = lax.dot_general(sel, h2_ref[...].astype(BF16), contract0, preferred_element_type=F32)

    lane = lax.broadcasted_iota(jnp.int32, route.shape, 1)
    gate_rows = jnp.zeros((LOCAL_ROWS, LANES), F32)
    for k, selk in ((0, sel1), (1, sel2)):
        pieces = _split3(route[:, ROUTE_W + k:ROUTE_W + k + 1])
        wp = jnp.zeros(route.shape, F32)
        for j, piece in enumerate(pieces):
            wp = jnp.where(lane == j, piece.astype(F32), wp)
        gate_rows += lax.dot_general(selk.astype(BF16), wp.astype(BF16), contract0,
                                     preferred_element_type=F32)
    buf[slot, :, d:] = gate_rows

    def push(s, wait):
        def fn(local_row, global_row):
            cp = pltpu.make_async_copy(buf.at[s, pl.ds(local_row, CHUNK), :],
                                       xs_hbm.at[pl.ds(global_row, CHUNK), :], sems.at[s])
            cp.wait() if wait else cp.start()
        return fn

    _for_each_unit(tile_units_ref, global_unit_ref, t, push(slot, False))

    @pl.when(t >= 1)
    def _():
        _for_each_unit(tile_units_ref, global_unit_ref, t - 1, push(1 - slot, True))

    @pl.when(t == n_tok_tiles - 1)
    def _():
        _for_each_unit(tile_units_ref, global_unit_ref, t, push(slot, True))
        zbuf[...] = jnp.zeros_like(zbuf)

        def tail_copy(e, bit):
            n = tailn_ref[e]
            done = (n >> (bit + 1)) << (bit + 1)
            row = pl.multiple_of((tailoff_ref[e] + done) * CHUNK, CHUNK)
            rows = (1 << bit) * CHUNK
            return pltpu.make_async_copy(zbuf.at[pl.ds(0, rows), :], xs_hbm.at[pl.ds(row, rows), :], zsem)

        def tile_copy(i):
            row = pl.multiple_of(i * TE, TE)
            return pltpu.make_async_copy(zbuf, xs_hbm.at[pl.ds(row, TE), :], zsem)

        for wait in (False, True):
            def tails(e, carry, wait=wait):
                for bit in reversed(range(TAIL_BITS)):
                    @pl.when(((tailn_ref[e] >> bit) & 1) == 1)
                    def _():
                        cp = tail_copy(e, bit)
                        cp.wait() if wait else cp.start()
                return carry
            lax.fori_loop(0, N_EXPERTS, tails, 0)

            def unused(i, carry, wait=wait):
                cp = tile_copy(i)
                cp.wait() if wait else cp.start()
                return carry
            lax.fori_loop(nt_ref[0], max_tiles, unused, 0)


def _dispatch(h2, route, plan, max_tiles):
    rows, d = h2.shape
    n_tok_tiles = rows // TM
    width = d + LANES
    return pl.pallas_call(
        functools.partial(_dispatch_kernel, d=d, n_tok_tiles=n_tok_tiles, max_tiles=max_tiles),
        grid_spec=pltpu.PrefetchScalarGridSpec(
            num_scalar_prefetch=5,
            grid=(n_tok_tiles,),
            in_specs=[
                pl.BlockSpec((TM, d), lambda t, *_: (t, 0)),
                pl.BlockSpec((TM, LANES), lambda t, *_: (t, 0)),
                pl.BlockSpec((None, 1, LANES), lambda t, *_: (t, 0, 0)),
            ],
            out_specs=pl.BlockSpec(memory_space=pl.ANY),
            scratch_shapes=[
                pltpu.VMEM((2, LOCAL_ROWS, width), F32),
                pltpu.VMEM((TE, width), F32),
                pltpu.SemaphoreType.DMA((2,)),
                pltpu.SemaphoreType.DMA(()),
            ],
        ),
        out_shape=jax.ShapeDtypeStruct((max_tiles * TE, width), F32),
        compiler_params=_cparams(("arbitrary",)),
        name="dispatch",
    )(plan["tile_units"], plan["global_unit"], plan["tail_units"], plan["tail_off"], plan["n_tiles"],
      h2, route, plan["local_off_rows"])


def _expert_kernel(te_ref, nt_ref, xs_ref, wg_ref, wu_ref, wd_ref, y_ref, wgu_b, wd_b, *, d, hidden):
    t = pl.program_id(0)
    nt = nt_ref[0]

    @pl.when(t < nt)
    def _():
        prev = te_ref[jnp.maximum(t - 1, 0)]

        @pl.when((t == 0) | (te_ref[t] != prev))
        def _():
            wgu_b[:, 0:hidden] = wg_ref[...].astype(BF16)
            wgu_b[:, hidden:2 * hidden] = wu_ref[...].astype(BF16)
            wd_b[...] = wd_ref[...].astype(BF16)

        x = xs_ref[:, 0:d].astype(BF16)
        gate = xs_ref[:, d:d + 1] + xs_ref[:, d + 1:d + 2] + xs_ref[:, d + 2:d + 3]
        au = jnp.dot(x, wgu_b[...], preferred_element_type=F32)
        a = au[:, 0:hidden]
        u = au[:, hidden:2 * hidden]
        act = (a * jax.nn.sigmoid(a) * u * gate).astype(BF16)
        y_ref[...] = jnp.dot(act, wd_b[...], preferred_element_type=F32)

    @pl.when(t >= nt)
    def _():
        y_ref[...] = jnp.zeros_like(y_ref)


def _experts(xs, plan, w_gate, w_up, w_down, max_tiles):
    width = xs.shape[1]
    d = width - LANES
    hidden = w_gate.shape[-1]
    last = lambda t, te, nt: jnp.minimum(t, nt[0] - 1)
    return pl.pallas_call(
        functools.partial(_expert_kernel, d=d, hidden=hidden),
        grid_spec=pltpu.PrefetchScalarGridSpec(
            num_scalar_prefetch=2,
            grid=(max_tiles,),
            in_specs=[
                pl.BlockSpec((TE, width), lambda t, te, nt: (last(t, te, nt), 0)),
                pl.BlockSpec((None, d, hidden), lambda t, te, nt: (te[last(t, te, nt)], 0, 0)),
                pl.BlockSpec((None, d, hidden), lambda t, te, nt: (te[last(t, te, nt)], 0, 0)),
                pl.BlockSpec((None, hidden, d), lambda t, te, nt: (te[last(t, te, nt)], 0, 0)),
            ],
            out_specs=pl.BlockSpec((TE, d), lambda t, te, nt: (t, 0)),
            scratch_shapes=[
                pltpu.VMEM((d, 2 * hidden), BF16),
                pltpu.VMEM((hidden, d), BF16),
            ],
        ),
        out_shape=jax.ShapeDtypeStruct((max_tiles * TE, d), F32),
        compiler_params=_cparams(("arbitrary",)),
        name="experts",
    )(plan["tile_expert"], plan["n_tiles"], xs, w_gate, w_up, w_down)


def _combine_kernel(tile_units_ref, global_unit_ref, y_hbm, x1_ref, route_ref, loff_ref, mod_ref, lng_ref,
                    lnb_ref, o_ref, ybuf, sems, *, d, n_tok_tiles):
    t = pl.program_id(0)
    slot = t % 2

    def pull(s, wait):
        def fn(local_row, global_row):
            cp = pltpu.make_async_copy(y_hbm.at[pl.ds(global_row, CHUNK), :],
                                       ybuf.at[s, pl.ds(local_row, CHUNK), :], sems.at[s])
            cp.wait() if wait else cp.start()
        return fn

    @pl.when(t == 0)
    def _():
        ybuf[...] = jnp.zeros_like(ybuf)
        _for_each_unit(tile_units_ref, global_unit_ref, 0, pull(0, False))

    @pl.when(t + 1 < n_tok_tiles)
    def _():
        _for_each_unit(tile_units_ref, global_unit_ref, t + 1, pull(1 - slot, False))

    _for_each_unit(tile_units_ref, global_unit_ref, t, pull(slot, True))

    pos1, pos2 = _local_positions(route_ref[...], loff_ref[...])
    sel = (_selection(pos1) | _selection(pos2)).astype(BF16)
    y = ybuf[slot]
    y_hi = y.astype(BF16)
    y_lo = (y - y_hi.astype(F32)).astype(BF16)
    fx = jnp.dot(sel, y_hi, preferred_element_type=F32) + jnp.dot(sel, y_lo, preferred_element_type=F32)
    gate_f = mod_ref[:, 5 * d:6 * d]
    o_ref[...] = _layer_norm(DEEPNORM_ALPHA * x1_ref[...] + gate_f * fx, lng_ref[...], lnb_ref[...])


def _combine(ys, plan, x1, route, mod, ln_g, ln_b, mod_tile_of):
    rows, d = x1.shape
    n_tok_tiles = rows // TM
    const = lambda t, *_: (0, 0)
    return pl.pallas_call(
        functools.partial(_combine_kernel, d=d, n_tok_tiles=n_tok_tiles),
        grid_spec=pltpu.PrefetchScalarGridSpec(
            num_scalar_prefetch=2,
            grid=(n_tok_tiles,),
            in_specs=[
                pl.BlockSpec(memory_space=pl.ANY),
                pl.BlockSpec((TM, d), lambda t, *_: (t, 0)),
                pl.BlockSpec((TM, LANES), lambda t, *_: (t, 0)),
                pl.BlockSpec((None, 1, LANES), lambda t, *_: (t, 0, 0)),
                pl.BlockSpec((None, 1, mod.shape[-1]), lambda t, *_: mod_tile_of(t)),
                pl.BlockSpec((1, d), const),
                pl.BlockSpec((1, d), const),
            ],
            out_specs=pl.BlockSpec((TM, d), lambda t, *_: (t, 0)),
            scratch_shapes=[
                pltpu.VMEM((2, LOCAL_ROWS, d), F32),
                pltpu.SemaphoreType.DMA((2,)),
            ],
        ),
        out_shape=jax.ShapeDtypeStruct((rows, d), F32),
        compiler_params=_cparams(("arbitrary",)),
        name="combine",
    )(plan["tile_units"], plan["global_unit"], ys, x1, route, plan["local_off_rows"], mod, ln_g, ln_b)


def kernel(x, c, ctx, c_ctx, w_mod, b_mod, ln_mix_g, ln_mix_b, ln_ffn_g, ln_ffn_b, diff_w_qkv, diff_w_o, diff_lambda_q1, diff_lambda_k1, diff_lambda_q2, diff_lambda_k2, diff_subln_g, gqa_w_qkv, gqa_w_o, gqa_q_norm_g, gqa_k_norm_g, moe_w_group, moe_b_group, moe_w_router, moe_b_router, moe_w_gate, moe_w_up, moe_w_down):
    batch, n, d = x.shape
    n_ctx = ctx.shape[1]
    t = n_ctx + n
    assert n % TM == 0 and n_ctx % TM == 0 and n % GRID_W == 0 and d % LANES == 0
    assert w_mod.shape[0] == DEPTH
    tiles_per_sample = t // TM
    ctx_tiles = n_ctx // TM
    q_tiles = tiles_per_sample - ctx_tiles

    pad = (-(batch + 1)) % SUBLANES
    cond = jnp.concatenate([c, c_ctx[None, :], jnp.zeros((pad, d), F32)], axis=0)
    ctx_row = batch
    mod_all = _modulation(cond, w_mod, b_mod)
    mod_all = mod_all.reshape(DEPTH, cond.shape[0], 1, N_MOD * d)

    xs = jnp.concatenate([ctx, x], axis=1).reshape(batch * t, d)

    for i in range(DEPTH):
        last = i == DEPTH - 1
        mod = mod_all[i]
        j = i // 2
        lng_m, lnb_m = ln_mix_g[i][None, :], ln_mix_b[i][None, :]
        lng_f, lnb_f = ln_ffn_g[i][None, :], ln_ffn_b[i][None, :]
        if i % 2 == 0:
            lambda_init = 0.8 - 0.6 * math.exp(-0.3 * i)
            tab = _rope_tables(n_ctx, n, DIFF_HEAD_DIM)
            q, k, v = _qkv_proj("diff", xs, mod, diff_w_qkv[j].astype(BF16), tab, None,
                                tiles_per_sample, ctx_tiles, ctx_row)
            lam_vecs = jnp.stack([diff_lambda_q1[j], diff_lambda_k1[j], diff_lambda_q2[j], diff_lambda_k2[j]])
            o_parts = _diff_attention(q, k, v, lam_vecs.astype(F32), diff_subln_g[j][None, :], batch, t, n_ctx,
                                      lambda_init)
            if last:
                o_parts = o_parts[1:]
            wo = diff_w_o[j]
        else:
            tab = _rope_tables(n_ctx, n, GQA_HEAD_DIM)
            q, k, v = _qkv_proj("gqa", xs, mod, gqa_w_qkv[j].astype(BF16), tab,
                                (gqa_q_norm_g[j][None, :], gqa_k_norm_g[j][None, :]),
                                tiles_per_sample, ctx_tiles, ctx_row)
            if not last:
                raise NotImplementedError("grouped-query layer with context outputs")
            o_parts = (_gqa_attention(q, k, v, batch, t, n_ctx),)
            wo = gqa_w_o[j]

        w_r = jnp.zeros((d, LANES), F32)
        w_r = w_r.at[:, :N_EXPERTS].set(moe_w_router[i]).at[:, GROUP_LANE0:GROUP_LANE0 + N_GROUPS].set(moe_w_group[i])
        b_r = jnp.zeros((1, LANES), F32)
        b_r = b_r.at[0, :N_EXPERTS].set(moe_b_router[i]).at[0, GROUP_LANE0:GROUP_LANE0 + N_GROUPS].set(moe_b_group[i])
        wr_hi = w_r.astype(BF16)
        wr_lo = (w_r - wr_hi.astype(F32)).astype(BF16)

        x1, h2, route, counts = _proj_route(o_parts, xs, mod, wo.astype(BF16), lng_m, lnb_m, wr_hi, wr_lo,
                                            b_r, tiles_per_sample, ctx_tiles, ctx_row)
        n_tok_tiles = x1.shape[0] // TM
        max_tiles = (2 * n_tok_tiles * TM + n_tok_tiles * N_EXPERTS * (CHUNK - 1)) // TE + N_EXPERTS
        plan = _dispatch_plan(counts, max_tiles, expert_base=i * N_EXPERTS)
        xsorted = _dispatch(h2, route, plan, max_tiles)
        hidden = moe_w_gate.shape[-1]
        w_gate = moe_w_gate.reshape(DEPTH * N_EXPERTS, d, hidden)
        w_up = moe_w_up.reshape(DEPTH * N_EXPERTS, d, hidden)
        w_down = moe_w_down.reshape(DEPTH * N_EXPERTS, hidden, d)
        ys = _experts(xsorted, plan, w_gate, w_up, w_down, max_tiles)

        row_map = _mod_row_map(tiles_per_sample, ctx_tiles, ctx_row)
        if last:
            mod_tile_of = lambda tt: row_map((tt // q_tiles) * tiles_per_sample + ctx_tiles + tt % q_tiles)
        else:
            mod_tile_of = row_map
        xs = _combine(ys, plan, x1, route, mod, lng_f, lnb_f, mod_tile_of)

    return xs.reshape(batch, n, d)
```

```python
import functools
import math

import jax
import jax.numpy as jnp
from jax import lax
from jax.experimental import pallas as pl
from jax.experimental.pallas import tpu as pltpu

F32 = jnp.float32
BF16 = jnp.bfloat16

GRID_W = 64
DIFF_HEAD_DIM = 64
GQA_HEAD_DIM = 128
GQA_KV_HEADS = 2
ROPE_THETA = 10000.0
N_GROUPS = 4
EXPERTS_PER_GROUP = 8
N_EXPERTS = N_GROUPS * EXPERTS_PER_GROUP
N_MOD = 6
LN_EPS = 1e-5
RMS_EPS = 1e-6
DEPTH = 2
DEEPNORM_ALPHA = (2 * DEPTH) ** 0.25
LOG2E = 1.4426950408889634

LANES = 128
SUBLANES = 8
TM = 256
TE = 512
ATTN_KC = 1280
ATTN_Q_TILES = 8
ATTN_SUB_TILES = 2
GQA_Q_TILES = 4
ROUTE_SUB_TILES = 2
VMEM_LIMIT = 48 * 1024 * 1024

GROUP_LANE0 = N_EXPERTS

CHUNK = SUBLANES
UNITS_PER_TILE = TE // CHUNK
TAIL_BITS = (UNITS_PER_TILE - 1).bit_length()
LOCAL_ROWS = 2 * TM + N_EXPERTS * CHUNK
LOCAL_UNITS = LOCAL_ROWS // CHUNK
ROUTE_E, ROUTE_W, ROUTE_RANK = 0, 2, 4


def _cparams(sem):
    return pltpu.CompilerParams(dimension_semantics=sem, vmem_limit_bytes=VMEM_LIMIT)


def _mod_kernel(c_ref, w_ref, b_ref, o_ref):
    c = c_ref[...]
    s = c * jax.nn.sigmoid(c)
    w = w_ref[...]
    sh = s.astype(BF16)
    sl = (s - sh.astype(F32)).astype(BF16)
    wh = w.astype(BF16)
    wl = (w - wh.astype(F32)).astype(BF16)
    acc = jnp.dot(sh, wh, preferred_element_type=F32)
    acc += jnp.dot(sl, wh, preferred_element_type=F32)
    acc += jnp.dot(sh, wl, preferred_element_type=F32)
    o_ref[...] = acc + b_ref[...]


def _modulation(cond, w_mod, b_mod):
    depth, d, width = w_mod.shape
    r = cond.shape[0]
    tn = 512
    return pl.pallas_call(
        _mod_kernel,
        grid=(depth, width // tn),
        in_specs=[
            pl.BlockSpec((r, d), lambda i, j: (0, 0)),
            pl.BlockSpec((None, d, tn), lambda i, j: (i, 0, j)),
            pl.BlockSpec((None, 1, tn), lambda i, j: (i, 0, j)),
        ],
        out_specs=pl.BlockSpec((None, r, tn), lambda i, j: (i, 0, j)),
        out_shape=jax.ShapeDtypeStruct((depth, r, width), F32),
        compiler_params=_cparams(("arbitrary", "arbitrary")),
        name="modulation",
    )(cond, w_mod, b_mod.reshape(depth, 1, width))


def _rope_tables(n_ctx, n, head_dim):
    rows = n // GRID_W
    row = jnp.broadcast_to(jnp.arange(rows, dtype=F32)[:, None], (rows, GRID_W)).reshape(-1)
    col = jnp.broadcast_to(jnp.arange(GRID_W, dtype=F32)[None, :], (rows, GRID_W)).reshape(-1)
    axis_dim = head_dim // 2
    inv_freq = ROPE_THETA ** (-jnp.arange(0, axis_dim, 2, dtype=F32) / axis_dim)
    ang = jnp.stack([row, col], axis=-1)[:, :, None] * inv_freq
    cos, sin = jnp.cos(ang), jnp.sin(ang)
    zero = jnp.zeros_like(sin)
    c = jnp.concatenate([cos, cos], axis=-1).reshape(n, head_dim)
    s_first = jnp.concatenate([-sin, zero], axis=-1).reshape(n, head_dim)
    s_second = jnp.concatenate([zero, sin], axis=-1).reshape(n, head_dim)
    tab = jnp.stack([c, s_first, s_second])
    tab = jnp.tile(tab, (1, 1, LANES // head_dim))
    ident = jnp.stack([jnp.ones((n_ctx, LANES), F32), jnp.zeros((n_ctx, LANES), F32),
                       jnp.zeros((n_ctx, LANES), F32)])
    return jnp.concatenate([ident, tab], axis=1)


def _rope_chunk(x, c, s_first, s_second, quarter):
    return x * c + pltpu.roll(x, LANES - quarter, 1) * s_first + pltpu.roll(x, quarter, 1) * s_second


def _qkv_diff_kernel(x_ref, mod_ref, w_ref, tab_ref, q_ref, k_ref, v_ref, *, d, qscale):
    x = x_ref[...]
    shift = mod_ref[:, 0:d]
    scale = mod_ref[:, d:2 * d]
    h = (x * (1.0 + scale) + shift).astype(BF16)
    qkv = jnp.dot(h, w_ref[...], preferred_element_type=F32)
    c, s_first, s_second = tab_ref[0], tab_ref[1], tab_ref[2]
    quarter = DIFF_HEAD_DIM // 4
    for j in range(d // LANES):
        lo, hi = j * LANES, (j + 1) * LANES
        q = _rope_chunk(qkv[:, lo:hi], c, s_first, s_second, quarter)
        q_ref[:, lo:hi] = (q * qscale).astype(BF16)
        k = _rope_chunk(qkv[:, d + lo:d + hi], c, s_first, s_second, quarter)
        k_ref[:, lo:hi] = k.astype(BF16)
    v_ref[...] = qkv[:, 2 * d:].astype(BF16)


def _rms_head(x, g):
    return x * lax.rsqrt(jnp.mean(x * x, axis=-1, keepdims=True) + RMS_EPS) * g


def _qkv_gqa_kernel(x_ref, mod_ref, w_ref, tab_ref, qg_ref, kg_ref, q_ref, k_ref, v_ref, *, d, qscale):
    x = x_ref[...]
    shift = mod_ref[:, 0:d]
    scale = mod_ref[:, d:2 * d]
    h = (x * (1.0 + scale) + shift).astype(BF16)
    qkv = jnp.dot(h, w_ref[...], preferred_element_type=F32)
    c, s_first, s_second = tab_ref[0], tab_ref[1], tab_ref[2]
    quarter = GQA_HEAD_DIM // 4
    kv_w = GQA_KV_HEADS * GQA_HEAD_DIM
    for j in range(d // LANES):
        lo, hi = j * LANES, (j + 1) * LANES
        q = _rope_chunk(_rms_head(qkv[:, lo:hi], qg_ref[...]), c, s_first, s_second, quarter)
        q_ref[:, lo:hi] = (q * qscale).astype(BF16)
    for j in range(GQA_KV_HEADS):
        lo, hi = j * LANES, (j + 1) * LANES
        k = _rope_chunk(_rms_head(qkv[:, d + lo:d + hi], kg_ref[...]), c, s_first, s_second, quarter)
        k_ref[:, lo:hi] = k.astype(BF16)
    v_ref[...] = qkv[:, d + kv_w:].astype(BF16)


def _mod_row_map(tiles_per_sample, ctx_tiles, ctx_row):
    def index(i):
        b = i // tiles_per_sample
        j = i % tiles_per_sample
        return (jnp.where(j < ctx_tiles, ctx_row, b), 0, 0)
    return index


def _qkv_proj(kind, xs, mod, w, tab, norm_g, tiles_per_sample, ctx_tiles, ctx_row):
    rows, d = xs.shape
    n_tiles = rows // TM
    width = w.shape[1]
    row_map = _mod_row_map(tiles_per_sample, ctx_tiles, ctx_row)
    in_specs = [
        pl.BlockSpec((TM, d), lambda i: (i, 0)),
        pl.BlockSpec((None, 1, mod.shape[-1]), row_map),
        pl.BlockSpec((d, width), lambda i: (0, 0)),
        pl.BlockSpec((3, TM, LANES), lambda i: (0, i % tiles_per_sample, 0)),
    ]
    args = [xs, mod, w, tab]
    if kind == "diff":
        kern = functools.partial(_qkv_diff_kernel, d=d, qscale=DIFF_HEAD_DIM ** -0.5 * LOG2E)
        kw, vw = d, d
    else:
        kern = functools.partial(_qkv_gqa_kernel, d=d, qscale=GQA_HEAD_DIM ** -0.5 * LOG2E)
        kw = vw = GQA_KV_HEADS * GQA_HEAD_DIM
        in_specs += [pl.BlockSpec((1, LANES), lambda i: (0, 0))] * 2
        args += list(norm_g)
    return pl.pallas_call(
        kern,
        grid=(n_tiles,),
        in_specs=in_specs,
        out_specs=[
            pl.BlockSpec((TM, d), lambda i: (i, 0)),
            pl.BlockSpec((TM, kw), lambda i: (i, 0)),
            pl.BlockSpec((TM, vw), lambda i: (i, 0)),
        ],
        out_shape=[
            jax.ShapeDtypeStruct((rows, d), BF16),
            jax.ShapeDtypeStruct((rows, kw), BF16),
            jax.ShapeDtypeStruct((rows, vw), BF16),
        ],
        compiler_params=_cparams(("arbitrary",)),
        name="qkv_" + kind,
    )(*args)


def _flash(q, k_ref, va_ref, n_keys):
    m = None
    acc = None
    for lo in range(0, n_keys, ATTN_KC):
        hi = min(lo + ATTN_KC, n_keys)
        s = lax.dot_general(q, k_ref[lo:hi, :], (((1,), (1,)), ((), ())), preferred_element_type=F32)
        m_new = jnp.max(s, axis=-1, keepdims=True)
        if m is not None:
            m_new = jnp.maximum(m, m_new)
        p = jnp.exp2(s - m_new).astype(BF16)
        pv = jnp.dot(p, va_ref[lo:hi, :], preferred_element_type=F32)
        acc = pv if acc is None else jnp.exp2(m - m_new) * acc + pv
        m = m_new
    return acc[:, 0:LANES] / acc[:, LANES:LANES + 1]


def _fill_values(v_ref, va_ref):
    va_ref[:, 0:LANES] = v_ref[...]
    lane = lax.broadcasted_iota(jnp.int32, v_ref.shape, 1)
    va_ref[:, LANES:2 * LANES] = jnp.where(lane == 0, 1.0, 0.0).astype(va_ref.dtype)


def _diff_attn_kernel(lam_ref, g_ref, *refs, n_q, lambda_init):
    q_refs = refs[:n_q]
    k_ref, v_ref, o_ref, va_ref = refs[n_q:]

    @pl.when(pl.program_id(2) == 0)
    def _():
        _fill_values(v_ref, va_ref)

    lv = lam_ref[...]
    lam = (jnp.exp(jnp.sum(lv[0:1] * lv[1:2], axis=-1, keepdims=True))
           - jnp.exp(jnp.sum(lv[2:3] * lv[3:4], axis=-1, keepdims=True)) + lambda_init)
    per = min(ATTN_SUB_TILES, n_q)
    for sub in range(n_q // per):
        parts = [r[...] for r in q_refs[sub * per:(sub + 1) * per]]
        q = jnp.concatenate(parts, axis=0) if per > 1 else parts[0]
        rows = q.shape[0]
        lane = lax.broadcasted_iota(jnp.int32, q.shape, 1)
        zero = jnp.zeros_like(q)
        q12 = jnp.concatenate([jnp.where(lane < DIFF_HEAD_DIM, q, zero),
                               jnp.where(lane >= DIFF_HEAD_DIM, q, zero)], axis=0)
        o12 = _flash(q12, k_ref, va_ref, k_ref.shape[0])
        o = o12[0:rows] - lam * o12[rows:2 * rows]
        o = o * lax.rsqrt(jnp.mean(o * o, axis=-1, keepdims=True) + RMS_EPS) * g_ref[...]
        o_ref[sub * rows:(sub + 1) * rows, :] = (o * (1.0 - lambda_init)).astype(o_ref.dtype)


def _diff_attention(q, k, v, lam_vecs, subln_g, batch, t, n_ctx, lambda_init):
    rows, d = q.shape
    heads = d // LANES
    tiles = t // TM
    ctx_tiles = n_ctx // TM
    q_steps = (tiles - ctx_tiles) // ATTN_Q_TILES
    assert t % n_ctx == 0 and (tiles - ctx_tiles) % ATTN_Q_TILES == 0
    const = lambda b, h, i: (0, 0)

    def call(name, n_q, q_tile_of, kv_rows, kv_block, out_rows, out_block_rows, out_index, steps):
        return pl.pallas_call(
            functools.partial(_diff_attn_kernel, n_q=n_q, lambda_init=lambda_init),
            grid=(batch, heads, steps),
            in_specs=[pl.BlockSpec(lam_vecs.shape, const), pl.BlockSpec((1, LANES), const)]
            + [pl.BlockSpec((TM, LANES), functools.partial(q_tile_of, s=s)) for s in range(n_q)]
            + [pl.BlockSpec((kv_rows, LANES), kv_block)] * 2,
            out_specs=pl.BlockSpec((out_block_rows, LANES), out_index),
            out_shape=jax.ShapeDtypeStruct((out_rows, d), BF16),
            scratch_shapes=[pltpu.VMEM((kv_rows, 2 * LANES), BF16)],
            compiler_params=_cparams(("arbitrary", "arbitrary", "arbitrary")),
            name=name,
        )(lam_vecs, subln_g, *([q] * n_q), k, v)

    o_ctx = call("diff_attention_ctx", 1, lambda b, h, i, s: (b * tiles + i, h),
                 n_ctx, lambda b, h, i: (b * (t // n_ctx), h),
                 batch * n_ctx, TM, lambda b, h, i: (b * ctx_tiles + i, h), ctx_tiles)
    o_lat = call("diff_attention", ATTN_Q_TILES,
                 lambda b, h, i, s: (b * tiles + ctx_tiles + ATTN_Q_TILES * i + s, h),
                 t, lambda b, h, i: (b, h),
                 batch * (t - n_ctx), ATTN_Q_TILES * TM, lambda b, h, i: (b * q_steps + i, h), q_steps)
    return o_ctx, o_lat


def _gqa_attn_kernel(*refs, n_q, group):
    q_refs = refs[:n_q]
    k_ref, v_ref, o_ref, va_ref = refs[n_q:]

    @pl.when(pl.program_id(2) == 0)
    def _():
        _fill_values(v_ref, va_ref)

    n_keys = k_ref.shape[0]
    for sub, q_ref in enumerate(q_refs):
        rows = q_ref.shape[0]
        q_all = jnp.concatenate([q_ref[:, g * LANES:(g + 1) * LANES] for g in range(group)], axis=0)
        o_all = _flash(q_all, k_ref, va_ref, n_keys)
        for g in range(group):
            o_ref[sub * rows:(sub + 1) * rows, g * LANES:(g + 1) * LANES] = (
                o_all[g * rows:(g + 1) * rows].astype(o_ref.dtype))


def _gqa_attention(q, k, v, batch, t, n_ctx):
    rows, d = q.shape
    group = d // GQA_HEAD_DIM // GQA_KV_HEADS
    tiles = t // TM
    ctx_tiles = n_ctx // TM
    q_tiles = tiles - ctx_tiles
    gw = group * LANES
    n_q = GQA_Q_TILES
    assert q_tiles % n_q == 0
    q_steps = q_tiles // n_q
    q_tile_of = lambda b, h, i, s: (b * tiles + ctx_tiles + n_q * i + s, h)
    return pl.pallas_call(
        functools.partial(_gqa_attn_kernel, n_q=n_q, group=group),
        grid=(batch, GQA_KV_HEADS, q_steps),
        in_specs=[pl.BlockSpec((TM, gw), functools.partial(q_tile_of, s=s)) for s in range(n_q)] + [
            pl.BlockSpec((t, LANES), lambda b, h, i: (b, h)),
            pl.BlockSpec((t, LANES), lambda b, h, i: (b, h)),
        ],
        out_specs=pl.BlockSpec((n_q * TM, gw), lambda b, h, i: (b * q_steps + i, h)),
        out_shape=jax.ShapeDtypeStruct((batch * q_tiles * TM, d), BF16),
        scratch_shapes=[pltpu.VMEM((t, 2 * LANES), BF16)],
        compiler_params=_cparams(("arbitrary", "arbitrary", "arbitrary")),
        name="gqa_attention",
    )(*([q] * n_q), k, v)


def _layer_norm(y, g, b):
    mu = jnp.mean(y, axis=-1, keepdims=True)
    yc = y - mu
    var = jnp.mean(yc * yc, axis=-1, keepdims=True)
    return yc * lax.rsqrt(var + LN_EPS) * g + b


def _proj_route_kernel(*refs, d, tiles_per_sample, ctx_tiles, with_ctx, n_sub):
    per = (2 if with_ctx else 1) + 2
    shared = refs[n_sub * per:n_sub * per + 6]
    outs = refs[n_sub * per + 6:]
    for sub in range(n_sub):
        tile_refs = refs[sub * per:(sub + 1) * per]
        if with_ctx:
            octx_ref, olat_ref = tile_refs[:2]
            tile = pl.program_id(0) * n_sub + sub
            tile_in_sample = jnp.full(olat_ref.shape, tile % tiles_per_sample, jnp.int32)
            o = jnp.where(tile_in_sample < ctx_tiles, octx_ref[...], olat_ref[...])
        else:
            o = tile_refs[0][...]
        _proj_route_tile(o, tile_refs[-2], tile_refs[-1], *shared, *outs, sub=sub, d=d)


def _proj_route_tile(o, x_ref, mod_ref, wo_ref, lng_ref, lnb_ref, wrh_ref, wrl_ref, br_ref,
                     x1_ref, h2_ref, route_ref, cnt_ref, *, sub, d):
    rows = slice(sub * TM, (sub + 1) * TM)
    gate_m = mod_ref[:, 2 * d:3 * d]
    shift_f = mod_ref[:, 3 * d:4 * d]
    scale_f = mod_ref[:, 4 * d:5 * d]
    ox = jnp.dot(o, wo_ref[...], preferred_element_type=F32)
    x1 = _layer_norm(DEEPNORM_ALPHA * x_ref[...] + gate_m * ox, lng_ref[...], lnb_ref[...])
    x1_ref[rows, :] = x1
    h2 = x1 * (1.0 + scale_f) + shift_f
    h2_ref[rows, :] = h2.astype(h2_ref.dtype)

    hh = h2.astype(BF16)
    hl = (h2 - hh.astype(F32)).astype(BF16)
    logits = jnp.dot(hh, wrh_ref[...], preferred_element_type=F32)
    logits += jnp.dot(hl, wrh_ref[...], preferred_element_type=F32)
    logits += jnp.dot(hh, wrl_ref[...], preferred_element_type=F32)
    logits += br_ref[...]

    lane = lax.broadcasted_iota(jnp.int32, logits.shape, 1).astype(F32)
    neg = jnp.full_like(logits, -jnp.inf)
    big = jnp.full_like(logits, 1e9)
    is_group = (lane >= GROUP_LANE0) & (lane < GROUP_LANE0 + N_GROUPS)
    lg = jnp.where(is_group, logits, neg)
    g_max = jnp.max(lg, axis=-1, keepdims=True)
    g_idx = jnp.min(jnp.where(lg == g_max, lane - GROUP_LANE0, big), axis=-1, keepdims=True)
    g_top = 1.0 / jnp.sum(jnp.exp(lg - g_max), axis=-1, keepdims=True)

    lane_group = jnp.floor(lane * (1.0 / EXPERTS_PER_GROUP))
    in_group = (lane < N_EXPERTS) & (lane_group == g_idx)
    le = jnp.where(in_group, logits, neg)
    m1 = jnp.max(le, axis=-1, keepdims=True)
    i1 = jnp.min(jnp.where(le == m1, lane, big), axis=-1, keepdims=True)
    le2 = jnp.where(lane == i1, neg, le)
    m2 = jnp.max(le2, axis=-1, keepdims=True)
    i2 = jnp.min(jnp.where(le2 == m2, lane, big), axis=-1, keepdims=True)
    r = jnp.exp(m2 - m1)
    w1 = g_top / (1.0 + r)
    w2 = g_top * r / (1.0 + r)

    a1 = (lane == i1).astype(F32)
    a2 = (lane == i2).astype(F32)
    both = (a1 + a2).astype(BF16)
    tm = logits.shape[0]
    rr = lax.broadcasted_iota(jnp.int32, (tm, tm), 0)
    cc = lax.broadcasted_iota(jnp.int32, (tm, tm), 1)
    strict_lower = (rr > cc).astype(BF16)
    before = jnp.dot(strict_lower, both, preferred_element_type=F32)
    rank1 = jnp.sum(a1 * before, axis=-1, keepdims=True)
    rank2 = jnp.sum(a2 * before, axis=-1, keepdims=True)
    cnt_ref[sub] = jnp.sum(a1 + a2, axis=0, keepdims=True)

    out = jnp.zeros_like(logits)
    for idx, val in ((ROUTE_E, i1), (ROUTE_E + 1, i2), (ROUTE_W, w1), (ROUTE_W + 1, w2),
                     (ROUTE_RANK, rank1), (ROUTE_RANK + 1, rank2)):
        out = jnp.where(lane == float(idx), val, out)
    route_ref[rows, :] = out


def _proj_route(o_parts, xs, mod, wo, ln_g, ln_b, wr_hi, wr_lo, br, tiles_per_sample, ctx_tiles, ctx_row):
    rows, d = xs.shape
    q_tiles = tiles_per_sample - ctx_tiles
    with_ctx = len(o_parts) == 2
    n_sub = ROUTE_SUB_TILES
    row_map = _mod_row_map(tiles_per_sample, ctx_tiles, ctx_row)
    if with_ctx:
        n_tiles = rows // TM
        tile_of = lambda i: i
        o_maps = [
            lambda i: ((i // tiles_per_sample) * ctx_tiles + jnp.minimum(i % tiles_per_sample, ctx_tiles - 1), 0),
            lambda i: ((i // tiles_per_sample) * q_tiles + jnp.maximum(i % tiles_per_sample - ctx_tiles, 0), 0),
        ]
    else:
        n_tiles = o_parts[0].shape[0] // TM
        tile_of = lambda i: (i // q_tiles) * tiles_per_sample + ctx_tiles + i % q_tiles
        o_maps = [lambda i: (i, 0)]
    assert n_tiles % n_sub == 0
    tile_specs, tile_args = [], []
    for sub in range(n_sub):
        at = lambda fn, sub=sub: (lambda step: fn(step * n_sub + sub))
        tile_specs += [pl.BlockSpec((TM, d), at(m)) for m in o_maps]
        tile_specs += [pl.BlockSpec((TM, d), at(lambda i: (tile_of(i), 0))),
                       pl.BlockSpec((None, 1, mod.shape[-1]), at(lambda i: row_map(tile_of(i))))]
        tile_args += [*o_parts, xs, mod]
    const = lambda i: (0, 0)
    return pl.pallas_call(
        functools.partial(_proj_route_kernel, d=d, tiles_per_sample=tiles_per_sample, ctx_tiles=ctx_tiles,
                          with_ctx=with_ctx, n_sub=n_sub),
        grid=(n_tiles // n_sub,),
        in_specs=tile_specs + [
            pl.BlockSpec((d, d), const),
            pl.BlockSpec((1, d), const),
            pl.BlockSpec((1, d), const),
            pl.BlockSpec((d, LANES), const),
            pl.BlockSpec((d, LANES), const),
            pl.BlockSpec((1, LANES), const),
        ],
        out_specs=[
            pl.BlockSpec((n_sub * TM, d), lambda i: (i, 0)),
            pl.BlockSpec((n_sub * TM, d), lambda i: (i, 0)),
            pl.BlockSpec((n_sub * TM, LANES), lambda i: (i, 0)),
            pl.BlockSpec((n_sub, 1, LANES), lambda i: (i, 0, 0)),
        ],
        out_shape=[
            jax.ShapeDtypeStruct((n_tiles * TM, d), F32),
            jax.ShapeDtypeStruct((n_tiles * TM, d), BF16),
            jax.ShapeDtypeStruct((n_tiles * TM, LANES), F32),
            jax.ShapeDtypeStruct((n_tiles, 1, LANES), F32),
        ],
        compiler_params=_cparams(("arbitrary",)),
        name="proj_route",
    )(*tile_args, wo, ln_g, ln_b, wr_hi, wr_lo, br)


def _dispatch_plan(counts, max_tiles, expert_base):
    n = counts[:, 0, :N_EXPERTS].astype(jnp.int32)
    units = (n + CHUNK - 1) // CHUNK
    local_off = jnp.cumsum(units, axis=1) - units
    total = jnp.sum(units, axis=0)
    tiles_e = (total + UNITS_PER_TILE - 1) // UNITS_PER_TILE
    tile_end = jnp.cumsum(tiles_e)
    region_off = (tile_end - tiles_e) * UNITS_PER_TILE
    base = region_off[None, :] + jnp.cumsum(units, axis=0) - units
    n_tiles = tile_end[-1:]
    tile_ids = jnp.arange(max_tiles, dtype=jnp.int32)
    tile_expert = jnp.sum((tile_end[None, :] <= tile_ids[:, None]).astype(jnp.int32), axis=1)
    tile_expert = jnp.minimum(tile_expert, N_EXPERTS - 1)
    tail_units = tiles_e * UNITS_PER_TILE - total
    tail_off = region_off + total
    local_off_rows = jnp.zeros((n.shape[0], 1, LANES), F32)
    local_off_rows = local_off_rows.at[:, 0, :N_EXPERTS].set((local_off * CHUNK).astype(F32))
    unit_ids = jnp.arange(LOCAL_UNITS, dtype=jnp.int32)
    local_end = local_off + units
    expert_of = jnp.sum((local_end[:, None, :] <= unit_ids[None, :, None]).astype(jnp.int32), axis=-1)
    onehot = (expert_of[:, :, None] == jnp.arange(N_EXPERTS, dtype=jnp.int32)).astype(jnp.int32)
    global_unit = jnp.sum(onehot * (base - local_off)[:, None, :], axis=-1) + unit_ids[None, :]
    i32 = lambda a: a.reshape(-1).astype(jnp.int32)
    return dict(tile_units=i32(jnp.sum(units, axis=1)), global_unit=i32(global_unit),
                tail_units=i32(tail_units), tail_off=i32(tail_off), n_tiles=i32(n_tiles),
                tile_expert=i32(tile_expert) + expert_base, local_off_rows=local_off_rows)


def _for_each_unit(tile_units_ref, global_unit_ref, tile, fn):
    def body(j, carry):
        g = global_unit_ref[tile * LOCAL_UNITS + j]
        fn(pl.multiple_of(j * CHUNK, CHUNK), pl.multiple_of(g * CHUNK, CHUNK))
        return carry
    lax.fori_loop(0, tile_units_ref[tile], body, 0)


def _local_positions(route, local_off_rows):
    lane = lax.broadcasted_iota(jnp.int32, route.shape, 1).astype(F32)
    pos = []
    for k in range(2):
        onehot = (lane == route[:, ROUTE_E + k:ROUTE_E + k + 1]).astype(F32)
        off = jnp.sum(onehot * local_off_rows, axis=-1, keepdims=True)
        pos.append(off + route[:, ROUTE_RANK + k:ROUTE_RANK + k + 1])
    return pos


def _selection(pos):
    slot = lax.broadcasted_iota(jnp.int32, (pos.shape[0], LOCAL_ROWS), 1).astype(F32)
    return slot == pos


def _split3(w):
    hi = w.astype(BF16)
    r1 = w - hi.astype(F32)
    mid = r1.astype(BF16)
    lo = (r1 - mid.astype(F32)).astype(BF16)
    return hi, mid, lo


def _dispatch_kernel(tile_units_ref, global_unit_ref, tailn_ref, tailoff_ref, nt_ref,
                     h2_ref, route_ref, loff_ref, xs_hbm, buf, zbuf, sems, zsem, *, d, n_tok_tiles, max_tiles):
    t = pl.program_id(0)
    slot = t % 2
    route = route_ref[...]
    pos1, pos2 = _local_positions(route, loff_ref[...])
    sel1 = _selection(pos1)
    sel2 = _selection(pos2)
    contract0 = (((0,), (0,)), ((), ()))
    sel = (sel1 | sel2).astype(BF16)
    buf[slot, :, 0:d] = lax.dot_general(sel, h2_ref[...], contract0, preferred_element_type=F32)

    lane = lax.broadcasted_iota(jnp.int32, route.shape, 1)
    gate_rows = jnp.zeros((LOCAL_ROWS, LANES), F32)
    for k, selk in ((0, sel1), (1, sel2)):
        pieces = _split3(route[:, ROUTE_W + k:ROUTE_W + k + 1])
        wp = jnp.zeros(route.shape, F32)
        for j, piece in enumerate(pieces):
            wp = jnp.where(lane == j, piece.astype(F32), wp)
        gate_rows += lax.dot_general(selk.astype(BF16), wp.astype(BF16), contract0,
                                     preferred_element_type=F32)
    buf[slot, :, d:] = gate_rows

    def push(s, wait):
        def fn(local_row, global_row):
            cp = pltpu.make_async_copy(buf.at[s, pl.ds(local_row, CHUNK), :],
                                       xs_hbm.at[pl.ds(global_row, CHUNK), :], sems.at[s])
            cp.wait() if wait else cp.start()
        return fn

    _for_each_unit(tile_units_ref, global_unit_ref, t, push(slot, False))

    @pl.when(t >= 1)
    def _():
        _for_each_unit(tile_units_ref, global_unit_ref, t - 1, push(1 - slot, True))

    @pl.when(t == n_tok_tiles - 1)
    def _():
        _for_each_unit(tile_units_ref, global_unit_ref, t, push(slot, True))
        zbuf[...] = jnp.zeros_like(zbuf)

        def tail_copy(e, bit):
            n = tailn_ref[e]
            done = (n >> (bit + 1)) << (bit + 1)
            row = pl.multiple_of((tailoff_ref[e] + done) * CHUNK, CHUNK)
            rows = (1 << bit) * CHUNK
            return pltpu.make_async_copy(zbuf.at[pl.ds(0, rows), :], xs_hbm.at[pl.ds(row, rows), :], zsem)

        def tile_copy(i):
            row = pl.multiple_of(i * TE, TE)
            return pltpu.make_async_copy(zbuf, xs_hbm.at[pl.ds(row, TE), :], zsem)

        for wait in (False, True):
            def tails(e, carry, wait=wait):
                for bit in reversed(range(TAIL_BITS)):
                    @pl.when(((tailn_ref[e] >> bit) & 1) == 1)
                    def _():
                        cp = tail_copy(e, bit)
                        cp.wait() if wait else cp.start()
                return carry
            lax.fori_loop(0, N_EXPERTS, tails, 0)

            def unused(i, carry, wait=wait):
                cp = tile_copy(i)
                cp.wait() if wait else cp.start()
                return carry
            lax.fori_loop(nt_ref[0], max_tiles, unused, 0)


def _dispatch(h2, route, plan, max_tiles):
    rows, d = h2.shape
    n_tok_tiles = rows // TM
    width = d + LANES
    return pl.pallas_call(
        functools.partial(_dispatch_kernel, d=d, n_tok_tiles=n_tok_tiles, max_tiles=max_tiles),
        grid_spec=pltpu.PrefetchScalarGridSpec(
            num_scalar_prefetch=5,
            grid=(n_tok_tiles,),
            in_specs=[
                pl.BlockSpec((TM, d), lambda t, *_: (t, 0)),
                pl.BlockSpec((TM, LANES), lambda t, *_: (t, 0)),
                pl.BlockSpec((None, 1, LANES), lambda t, *_: (t, 0, 0)),
            ],
            out_specs=pl.BlockSpec(memory_space=pl.ANY),
            scratch_shapes=[
                pltpu.VMEM((2, LOCAL_ROWS, width), F32),
                pltpu.VMEM((TE, width), F32),
                pltpu.SemaphoreType.DMA((2,)),
                pltpu.SemaphoreType.DMA(()),
            ],
        ),
        out_shape=jax.ShapeDtypeStruct((max_tiles * TE, width), F32),
        compiler_params=_cparams(("arbitrary",)),
        name="dispatch",
    )(plan["tile_units"], plan["global_unit"], plan["tail_units"], plan["tail_off"], plan["n_tiles"],
      h2, route, plan["local_off_rows"])


def _expert_kernel(te_ref, nt_ref, xs_ref, wg_ref, wu_ref, wd_ref, y_ref, wgu_b, wd_b, *, d, hidden):
    t = pl.program_id(0)
    nt = nt_ref[0]

    @pl.when(t < nt)
    def _():
        prev = te_ref[jnp.maximum(t - 1, 0)]

        @pl.when((t == 0) | (te_ref[t] != prev))
        def _():
            wgu_b[:, 0:hidden] = wg_ref[...].astype(BF16)
            wgu_b[:, hidden:2 * hidden] = wu_ref[...].astype(BF16)
            wd_b[...] = wd_ref[...].astype(BF16)

        x = xs_ref[:, 0:d].astype(BF16)
        gate = xs_ref[:, d:d + 1] + xs_ref[:, d + 1:d + 2] + xs_ref[:, d + 2:d + 3]
        au = jnp.dot(x, wgu_b[...], preferred_element_type=F32)
        a = au[:, 0:hidden]
        u = au[:, hidden:2 * hidden]
        act = (a * jax.nn.sigmoid(a) * u * gate).astype(BF16)
        y_ref[...] = jnp.dot(act, wd_b[...], preferred_element_type=F32)

    @pl.when(t >= nt)
    def _():
        y_ref[...] = jnp.zeros_like(y_ref)


def _experts(xs, plan, w_gate, w_up, w_down, max_tiles):
    width = xs.shape[1]
    d = width - LANES
    hidden = w_gate.shape[-1]
    last = lambda t, te, nt: jnp.minimum(t, nt[0] - 1)
    return pl.pallas_call(
        functools.partial(_expert_kernel, d=d, hidden=hidden),
        grid_spec=pltpu.PrefetchScalarGridSpec(
            num_scalar_prefetch=2,
            grid=(max_tiles,),
            in_specs=[
                pl.BlockSpec((TE, width), lambda t, te, nt: (last(t, te, nt), 0)),
                pl.BlockSpec((None, d, hidden), lambda t, te, nt: (te[last(t, te, nt)], 0, 0)),
                pl.BlockSpec((None, d, hidden), lambda t, te, nt: (te[last(t, te, nt)], 0, 0)),
                pl.BlockSpec((None, hidden, d), lambda t, te, nt: (te[last(t, te, nt)], 0, 0)),
            ],
            out_specs=pl.BlockSpec((TE, d), lambda t, te, nt: (t, 0)),
            scratch_shapes=[
                pltpu.VMEM((d, 2 * hidden), BF16),
                pltpu.VMEM((hidden, d), BF16),
            ],
        ),
        out_shape=jax.ShapeDtypeStruct((max_tiles * TE, d), F32),
        compiler_params=_cparams(("arbitrary",)),
        name="experts",
    )(plan["tile_expert"], plan["n_tiles"], xs, w_gate, w_up, w_down)


def _combine_kernel(tile_units_ref, global_unit_ref, y_hbm, x1_ref, route_ref, loff_ref, mod_ref, lng_ref,
                    lnb_ref, o_ref, ybuf, sems, *, d, n_tok_tiles):
    t = pl.program_id(0)
    slot = t % 2

    def pull(s, wait):
        def fn(local_row, global_row):
            cp = pltpu.make_async_copy(y_hbm.at[pl.ds(global_row, CHUNK), :],
                                       ybuf.at[s, pl.ds(local_row, CHUNK), :], sems.at[s])
            cp.wait() if wait else cp.start()
        return fn

    @pl.when(t == 0)
    def _():
        ybuf[...] = jnp.zeros_like(ybuf)
        _for_each_unit(tile_units_ref, global_unit_ref, 0, pull(0, False))

    @pl.when(t + 1 < n_tok_tiles)
    def _():
        _for_each_unit(tile_units_ref, global_unit_ref, t + 1, pull(1 - slot, False))

    _for_each_unit(tile_units_ref, global_unit_ref, t, pull(slot, True))

    pos1, pos2 = _local_positions(route_ref[...], loff_ref[...])
    sel = (_selection(pos1) | _selection(pos2)).astype(BF16)
    y = ybuf[slot]
    y_hi = y.astype(BF16)
    y_lo = (y - y_hi.astype(F32)).astype(BF16)
    fx = jnp.dot(sel, y_hi, preferred_element_type=F32) + jnp.dot(sel, y_lo, preferred_element_type=F32)
    gate_f = mod_ref[:, 5 * d:6 * d]
    o_ref[...] = _layer_norm(DEEPNORM_ALPHA * x1_ref[...] + gate_f * fx, lng_ref[...], lnb_ref[...])


def _combine(ys, plan, x1, route, mod, ln_g, ln_b, mod_tile_of):
    rows, d = x1.shape
    n_tok_tiles = rows // TM
    const = lambda t, *_: (0, 0)
    return pl.pallas_call(
        functools.partial(_combine_kernel, d=d, n_tok_tiles=n_tok_tiles),
        grid_spec=pltpu.PrefetchScalarGridSpec(
            num_scalar_prefetch=2,
            grid=(n_tok_tiles,),
            in_specs=[
                pl.BlockSpec(memory_space=pl.ANY),
                pl.BlockSpec((TM, d), lambda t, *_: (t, 0)),
                pl.BlockSpec((TM, LANES), lambda t, *_: (t, 0)),
                pl.BlockSpec((None, 1, LANES), lambda t, *_: (t, 0, 0)),
                pl.BlockSpec((None, 1, mod.shape[-1]), lambda t, *_: mod_tile_of(t)),
                pl.BlockSpec((1, d), const),
                pl.BlockSpec((1, d), const),
            ],
            out_specs=pl.BlockSpec((TM, d), lambda t, *_: (t, 0)),
            scratch_shapes=[
                pltpu.VMEM((2, LOCAL_ROWS, d), F32),
                pltpu.SemaphoreType.DMA((2,)),
            ],
        ),
        out_shape=jax.ShapeDtypeStruct((rows, d), F32),
        compiler_params=_cparams(("arbitrary",)),
        name="combine",
    )(plan["tile_units"], plan["global_unit"], ys, x1, route, plan["local_off_rows"], mod, ln_g, ln_b)


def kernel(x, c, ctx, c_ctx, w_mod, b_mod, ln_mix_g, ln_mix_b, ln_ffn_g, ln_ffn_b, diff_w_qkv, diff_w_o, diff_lambda_q1, diff_lambda_k1, diff_lambda_q2, diff_lambda_k2, diff_subln_g, gqa_w_qkv, gqa_w_o, gqa_q_norm_g, gqa_k_norm_g, moe_w_group, moe_b_group, moe_w_router, moe_b_router, moe_w_gate, moe_w_up, moe_w_down):
    batch, n, d = x.shape
    n_ctx = ctx.shape[1]
    t = n_ctx + n
    assert n % TM == 0 and n_ctx % TM == 0 and n % GRID_W == 0 and d % LANES == 0
    assert w_mod.shape[0] == DEPTH
    tiles_per_sample = t // TM
    ctx_tiles = n_ctx // TM
    q_tiles = tiles_per_sample - ctx_tiles

    pad = (-(batch + 1)) % SUBLANES
    cond = jnp.concatenate([c, c_ctx[None, :], jnp.zeros((pad, d), F32)], axis=0)
    ctx_row = batch
    mod_all = _modulation(cond, w_mod, b_mod)
    mod_all = mod_all.reshape(DEPTH, cond.shape[0], 1, N_MOD * d)

    xs = jnp.concatenate([ctx, x], axis=1).reshape(batch * t, d)

    for i in range(DEPTH):
        last = i == DEPTH - 1
        mod = mod_all[i]
        j = i // 2
        lng_m, lnb_m = ln_mix_g[i][None, :], ln_mix_b[i][None, :]
        lng_f, lnb_f = ln_ffn_g[i][None, :], ln_ffn_b[i][None, :]
        if i % 2 == 0:
            lambda_init = 0.8 - 0.6 * math.exp(-0.3 * i)
            tab = _rope_tables(n_ctx, n, DIFF_HEAD_DIM)
            q, k, v = _qkv_proj("diff", xs, mod, diff_w_qkv[j].astype(BF16), tab, None,
                                tiles_per_sample, ctx_tiles, ctx_row)
            lam_vecs = jnp.stack([diff_lambda_q1[j], diff_lambda_k1[j], diff_lambda_q2[j], diff_lambda_k2[j]])
            o_parts = _diff_attention(q, k, v, lam_vecs.astype(F32), diff_subln_g[j][None, :], batch, t, n_ctx,
                                      lambda_init)
            if last:
                o_parts = o_parts[1:]
            wo = diff_w_o[j]
        else:
            tab = _rope_tables(n_ctx, n, GQA_HEAD_DIM)
            q, k, v = _qkv_proj("gqa", xs, mod, gqa_w_qkv[j].astype(BF16), tab,
                                (gqa_q_norm_g[j][None, :], gqa_k_norm_g[j][None, :]),
                                tiles_per_sample, ctx_tiles, ctx_row)
            if not last:
                raise NotImplementedError("grouped-query layer with context outputs")
            o_parts = (_gqa_attention(q, k, v, batch, t, n_ctx),)
            wo = gqa_w_o[j]

        w_r = jnp.zeros((d, LANES), F32)
        w_r = w_r.at[:, :N_EXPERTS].set(moe_w_router[i]).at[:, GROUP_LANE0:GROUP_LANE0 + N_GROUPS].set(moe_w_group[i])
        b_r = jnp.zeros((1, LANES), F32)
        b_r = b_r.at[0, :N_EXPERTS].set(moe_b_router[i]).at[0, GROUP_LANE0:GROUP_LANE0 + N_GROUPS].set(moe_b_group[i])
        wr_hi = w_r.astype(BF16)
        wr_lo = (w_r - wr_hi.astype(F32)).astype(BF16)

        x1, h2, route, counts = _proj_route(o_parts, xs, mod, wo.astype(BF16), lng_m, lnb_m, wr_hi, wr_lo,
                                            b_r, tiles_per_sample, ctx_tiles, ctx_row)
        n_tok_tiles = x1.shape[0] // TM
        max_tiles = (2 * n_tok_tiles * TM + n_tok_tiles * N_EXPERTS * (CHUNK - 1)) // TE + N_EXPERTS
        plan = _dispatch_plan(counts, max_tiles, expert_base=i * N_EXPERTS)
        xsorted = _dispatch(h2, route, plan, max_tiles)
        hidden = moe_w_gate.shape[-1]
        w_gate = moe_w_gate.reshape(DEPTH * N_EXPERTS, d, hidden)
        w_up = moe_w_up.reshape(DEPTH * N_EXPERTS, d, hidden)
        w_down = moe_w_down.reshape(DEPTH * N_EXPERTS, hidden, d)
        ys = _experts(xsorted, plan, w_gate, w_up, w_down, max_tiles)

        row_map = _mod_row_map(tiles_per_sample, ctx_tiles, ctx_row)
        if last:
            mod_tile_of = lambda tt: row_map((tt // q_tiles) * tiles_per_sample + ctx_tiles + tt % q_tiles)
        else:
            mod_tile_of = row_map
        xs = _combine(ys, plan, x1, route, mod, lng_f, lnb_f, mod_tile_of)

    return xs.reshape(batch, n, d)
```

```python
import functools
import math

import jax
import jax.numpy as jnp
from jax import lax
from jax.experimental import pallas as pl
from jax.experimental.pallas import tpu as pltpu

F32 = jnp.float32
BF16 = jnp.bfloat16

GRID_W = 64
DIFF_HEAD_DIM = 64
GQA_HEAD_DIM = 128
GQA_KV_HEADS = 2
ROPE_THETA = 10000.0
N_GROUPS = 4
EXPERTS_PER_GROUP = 8
N_EXPERTS = N_GROUPS * EXPERTS_PER_GROUP
N_MOD = 6
LN_EPS = 1e-5
RMS_EPS = 1e-6
DEPTH = 2
DEEPNORM_ALPHA = (2 * DEPTH) ** 0.25
LOG2E = 1.4426950408889634

LANES = 128
SUBLANES = 8
TM = 256
TE = 512
ATTN_KC = 1280
ATTN_Q_TILES = 8
ATTN_SUB_TILES = 2
GQA_Q_TILES = 4
ROUTE_SUB_TILES = 2
VMEM_LIMIT = 48 * 1024 * 1024

GROUP_LANE0 = N_EXPERTS

CHUNK = SUBLANES
UNITS_PER_TILE = TE // CHUNK
TAIL_BITS = (UNITS_PER_TILE - 1).bit_length()
LOCAL_ROWS = 2 * TM + N_EXPERTS * CHUNK
LOCAL_UNITS = LOCAL_ROWS // CHUNK
ROUTE_E, ROUTE_W, ROUTE_RANK = 0, 2, 4


def _cparams(sem):
    return pltpu.CompilerParams(dimension_semantics=sem, vmem_limit_bytes=VMEM_LIMIT)


def _mod_kernel(c_ref, w_ref, b_ref, o_ref):
    c = c_ref[...]
    s = c * jax.nn.sigmoid(c)
    w = w_ref[...]
    sh = s.astype(BF16)
    sl = (s - sh.astype(F32)).astype(BF16)
    wh = w.astype(BF16)
    wl = (w - wh.astype(F32)).astype(BF16)
    acc = jnp.dot(sh, wh, preferred_element_type=F32)
    acc += jnp.dot(sl, wh, preferred_element_type=F32)
    acc += jnp.dot(sh, wl, preferred_element_type=F32)
    o_ref[...] = acc + b_ref[...]


def _modulation(cond, w_mod, b_mod):
    depth, d, width = w_mod.shape
    r = cond.shape[0]
    tn = 512
    return pl.pallas_call(
        _mod_kernel,
        grid=(depth, width // tn),
        in_specs=[
            pl.BlockSpec((r, d), lambda i, j: (0, 0)),
            pl.BlockSpec((None, d, tn), lambda i, j: (i, 0, j)),
            pl.BlockSpec((None, 1, tn), lambda i, j: (i, 0, j)),
        ],
        out_specs=pl.BlockSpec((None, r, tn), lambda i, j: (i, 0, j)),
        out_shape=jax.ShapeDtypeStruct((depth, r, width), F32),
        compiler_params=_cparams(("arbitrary", "arbitrary")),
        name="modulation",
    )(cond, w_mod, b_mod.reshape(depth, 1, width))


def _rope_tables(n_ctx, n, head_dim):
    rows = n // GRID_W
    row = jnp.broadcast_to(jnp.arange(rows, dtype=F32)[:, None], (rows, GRID_W)).reshape(-1)
    col = jnp.broadcast_to(jnp.arange(GRID_W, dtype=F32)[None, :], (rows, GRID_W)).reshape(-1)
    axis_dim = head_dim // 2
    inv_freq = ROPE_THETA ** (-jnp.arange(0, axis_dim, 2, dtype=F32) / axis_dim)
    ang = jnp.stack([row, col], axis=-1)[:, :, None] * inv_freq
    cos, sin = jnp.cos(ang), jnp.sin(ang)
    zero = jnp.zeros_like(sin)
    c = jnp.concatenate([cos, cos], axis=-1).reshape(n, head_dim)
    s_first = jnp.concatenate([-sin, zero], axis=-1).reshape(n, head_dim)
    s_second = jnp.concatenate([zero, sin], axis=-1).reshape(n, head_dim)
    tab = jnp.stack([c, s_first, s_second])
    tab = jnp.tile(tab, (1, 1, LANES // head_dim))
    ident = jnp.stack([jnp.ones((n_ctx, LANES), F32), jnp.zeros((n_ctx, LANES), F32),
                       jnp.zeros((n_ctx, LANES), F32)])
    return jnp.concatenate([ident, tab], axis=1)


def _rope_chunk(x, c, s_first, s_second, quarter):
    return x * c + pltpu.roll(x, LANES - quarter, 1) * s_first + pltpu.roll(x, quarter, 1) * s_second


def _qkv_diff_body(x, mod_ref, w_ref, tab_ref, q_ref, k_ref, v_ref, *, d, qscale):
    shift = mod_ref[:, 0:d]
    scale = mod_ref[:, d:2 * d]
    h = (x * (1.0 + scale) + shift).astype(BF16)
    qkv = jnp.dot(h, w_ref[...], preferred_element_type=F32)
    c, s_first, s_second = tab_ref[0], tab_ref[1], tab_ref[2]
    quarter = DIFF_HEAD_DIM // 4
    for j in range(d // LANES):
        lo, hi = j * LANES, (j + 1) * LANES
        q = _rope_chunk(qkv[:, lo:hi], c, s_first, s_second, quarter)
        q_ref[:, lo:hi] = (q * qscale).astype(BF16)
        k = _rope_chunk(qkv[:, d + lo:d + hi], c, s_first, s_second, quarter)
        k_ref[:, lo:hi] = k.astype(BF16)
    v_ref[...] = qkv[:, 2 * d:].astype(BF16)


def _rms_head(x, g):
    return x * lax.rsqrt(jnp.mean(x * x, axis=-1, keepdims=True) + RMS_EPS) * g


def _qkv_gqa_body(x, mod_ref, w_ref, tab_ref, qg_ref, kg_ref, q_ref, k_ref, v_ref, *, d, qscale):
    shift = mod_ref[:, 0:d]
    scale = mod_ref[:, d:2 * d]
    h = (x * (1.0 + scale) + shift).astype(BF16)
    qkv = jnp.dot(h, w_ref[...], preferred_element_type=F32)
    c, s_first, s_second = tab_ref[0], tab_ref[1], tab_ref[2]
    quarter = GQA_HEAD_DIM // 4
    kv_w = GQA_KV_HEADS * GQA_HEAD_DIM
    for j in range(d // LANES):
        lo, hi = j * LANES, (j + 1) * LANES
        q = _rope_chunk(_rms_head(qkv[:, lo:hi], qg_ref[...]), c, s_first, s_second, quarter)
        q_ref[:, lo:hi] = (q * qscale).astype(BF16)
    for j in range(GQA_KV_HEADS):
        lo, hi = j * LANES, (j + 1) * LANES
        k = _rope_chunk(_rms_head(qkv[:, d + lo:d + hi], kg_ref[...]), c, s_first, s_second, quarter)
        k_ref[:, lo:hi] = k.astype(BF16)
    v_ref[...] = qkv[:, d + kv_w:].astype(BF16)


def _mod_row_map(tiles_per_sample, ctx_tiles, ctx_row):
    def index(i):
        b = i // tiles_per_sample
        j = i % tiles_per_sample
        return (jnp.where(j < ctx_tiles, ctx_row, b), 0, 0)
    return index


def _merge_parts(part_refs, tile, tiles_per_sample, ctx_tiles):
    if len(part_refs) == 1:
        return part_refs[0][...]
    ctx_ref, lat_ref = part_refs
    tile_in_sample = jnp.full(lat_ref.shape, tile % tiles_per_sample, jnp.int32)
    return jnp.where(tile_in_sample < ctx_tiles, ctx_ref[...], lat_ref[...])


def _part_maps(n_parts, tiles_per_sample, ctx_tiles, tile_of):
    if n_parts == 1:
        return [lambda i: (tile_of(i), 0)]
    q_tiles = tiles_per_sample - ctx_tiles
    return [
        lambda i: ((i // tiles_per_sample) * ctx_tiles + jnp.minimum(i % tiles_per_sample, ctx_tiles - 1), 0),
        lambda i: ((i // tiles_per_sample) * q_tiles + jnp.maximum(i % tiles_per_sample - ctx_tiles, 0), 0),
    ]


def _qkv_kernel(*refs, body, n_parts, tiles_per_sample, ctx_tiles):
    x = _merge_parts(refs[:n_parts], pl.program_id(0), tiles_per_sample, ctx_tiles)
    body(x, *refs[n_parts:])


def _qkv_proj(kind, x_parts, mod, w, tab, norm_g, tiles_per_sample, ctx_tiles, ctx_row):
    d = x_parts[0].shape[1]
    rows = sum(p.shape[0] for p in x_parts)
    n_tiles = rows // TM
    width = w.shape[1]
    row_map = _mod_row_map(tiles_per_sample, ctx_tiles, ctx_row)
    in_specs = [pl.BlockSpec((TM, d), m)
                for m in _part_maps(len(x_parts), tiles_per_sample, ctx_tiles, lambda i: i)] + [
        pl.BlockSpec((None, 1, mod.shape[-1]), row_map),
        pl.BlockSpec((d, width), lambda i: (0, 0)),
        pl.BlockSpec((3, TM, LANES), lambda i: (0, i % tiles_per_sample, 0)),
    ]
    args = [*x_parts, mod, w, tab]
    if kind == "diff":
        body = functools.partial(_qkv_diff_body, d=d, qscale=DIFF_HEAD_DIM ** -0.5 * LOG2E)
        kw, vw = d, d
    else:
        body = functools.partial(_qkv_gqa_body, d=d, qscale=GQA_HEAD_DIM ** -0.5 * LOG2E)
        kw = vw = GQA_KV_HEADS * GQA_HEAD_DIM
        in_specs += [pl.BlockSpec((1, LANES), lambda i: (0, 0))] * 2
        args += list(norm_g)
    kern = functools.partial(_qkv_kernel, body=body, n_parts=len(x_parts), tiles_per_sample=tiles_per_sample,
                             ctx_tiles=ctx_tiles)
    return pl.pallas_call(
        kern,
        grid=(n_tiles,),
        in_specs=in_specs,
        out_specs=[
            pl.BlockSpec((TM, d), lambda i: (i, 0)),
            pl.BlockSpec((TM, kw), lambda i: (i, 0)),
            pl.BlockSpec((TM, vw), lambda i: (i, 0)),
        ],
        out_shape=[
            jax.ShapeDtypeStruct((rows, d), BF16),
            jax.ShapeDtypeStruct((rows, kw), BF16),
            jax.ShapeDtypeStruct((rows, vw), BF16),
        ],
        compiler_params=_cparams(("arbitrary",)),
        name="qkv_" + kind,
    )(*args)


def _flash(q, k_ref, va_ref, n_keys):
    m = None
    acc = None
    for lo in range(0, n_keys, ATTN_KC):
        hi = min(lo + ATTN_KC, n_keys)
        s = lax.dot_general(q, k_ref[lo:hi, :], (((1,), (1,)), ((), ())), preferred_element_type=F32)
        m_new = jnp.max(s, axis=-1, keepdims=True)
        if m is not None:
            m_new = jnp.maximum(m, m_new)
        p = jnp.exp2(s - m_new).astype(BF16)
        pv = jnp.dot(p, va_ref[lo:hi, :], preferred_element_type=F32)
        acc = pv if acc is None else jnp.exp2(m - m_new) * acc + pv
        m = m_new
    return acc[:, 0:LANES] / acc[:, LANES:LANES + 1]


def _fill_values(v_ref, va_ref):
    va_ref[:, 0:LANES] = v_ref[...]
    lane = lax.broadcasted_iota(jnp.int32, v_ref.shape, 1)
    va_ref[:, LANES:2 * LANES] = jnp.where(lane == 0, 1.0, 0.0).astype(va_ref.dtype)


def _diff_lambda(lam_ref, lambda_init):
    lv = lam_ref[...]
    return (jnp.exp(jnp.sum(lv[0:1] * lv[1:2], axis=-1, keepdims=True))
            - jnp.exp(jnp.sum(lv[2:3] * lv[3:4], axis=-1, keepdims=True)) + lambda_init)


def _diff_head(q, lam, g, k_ref, va_ref, lambda_init):
    rows = q.shape[0]
    lane = lax.broadcasted_iota(jnp.int32, q.shape, 1)
    zero = jnp.zeros_like(q)
    q12 = jnp.concatenate([jnp.where(lane < DIFF_HEAD_DIM, q, zero),
                           jnp.where(lane >= DIFF_HEAD_DIM, q, zero)], axis=0)
    o12 = _flash(q12, k_ref, va_ref, k_ref.shape[0])
    o = o12[0:rows] - lam * o12[rows:2 * rows]
    o = o * lax.rsqrt(jnp.mean(o * o, axis=-1, keepdims=True) + RMS_EPS) * g
    return o * (1.0 - lambda_init)


def _diff_attn_kernel(lam_ref, g_ref, *refs, n_q, lambda_init):
    q_refs = refs[:n_q]
    k_ref, v_ref, o_ref, va_ref = refs[n_q:]

    @pl.when(pl.program_id(2) == 0)
    def _():
        _fill_values(v_ref, va_ref)

    lam = _diff_lambda(lam_ref, lambda_init)
    per = min(ATTN_SUB_TILES, n_q)
    for sub in range(n_q // per):
        parts = [r[...] for r in q_refs[sub * per:(sub + 1) * per]]
        q = jnp.concatenate(parts, axis=0) if per > 1 else parts[0]
        rows = q.shape[0]
        o = _diff_head(q, lam, g_ref[...], k_ref, va_ref, lambda_init)
        o_ref[sub * rows:(sub + 1) * rows, :] = o.astype(o_ref.dtype)


def _diff_ctx_kernel(lam_ref, g_ref, q_ref, k_ref, v_ref, o_ref, va_ref, *, heads, lambda_init):
    head = lambda ref, h: ref.at[:, pl.ds(h * LANES, LANES)]

    @pl.when(pl.program_id(1) == 0)
    def _():
        for h in range(heads):
            _fill_values(head(v_ref, h), va_ref.at[h])

    lam = _diff_lambda(lam_ref, lambda_init)
    for h in range(heads):
        o = _diff_head(q_ref[:, h * LANES:(h + 1) * LANES], lam, g_ref[...], head(k_ref, h), va_ref.at[h],
                       lambda_init)
        o_ref[:, h * LANES:(h + 1) * LANES] = o.astype(o_ref.dtype)


def _diff_attention(q, k, v, lam_vecs, subln_g, batch, t, n_ctx, lambda_init):
    rows, d = q.shape
    heads = d // LANES
    tiles = t // TM
    ctx_tiles = n_ctx // TM
    q_steps = (tiles - ctx_tiles) // ATTN_Q_TILES
    assert t % n_ctx == 0 and (tiles - ctx_tiles) % ATTN_Q_TILES == 0
    const = lambda b, h, i: (0, 0)

    def call(name, n_q, q_tile_of, kv_rows, kv_block, out_rows, out_block_rows, out_index, steps):
        return pl.pallas_call(
            functools.partial(_diff_attn_kernel, n_q=n_q, lambda_init=lambda_init),
            grid=(batch, heads, steps),
            in_specs=[pl.BlockSpec(lam_vecs.shape, const), pl.BlockSpec((1, LANES), const)]
            + [pl.BlockSpec((TM, LANES), functools.partial(q_tile_of, s=s)) for s in range(n_q)]
            + [pl.BlockSpec((kv_rows, LANES), kv_block)] * 2,
            out_specs=pl.BlockSpec((out_block_rows, LANES), out_index),
            out_shape=jax.ShapeDtypeStruct((out_rows, d), BF16),
            scratch_shapes=[pltpu.VMEM((kv_rows, 2 * LANES), BF16)],
            compiler_params=_cparams(("arbitrary", "arbitrary", "arbitrary")),
            name=name,
        )(lam_vecs, subln_g, *([q] * n_q), k, v)

    o_ctx = pl.pallas_call(
        functools.partial(_diff_ctx_kernel, heads=heads, lambda_init=lambda_init),
        grid=(batch, ctx_tiles),
        in_specs=[
            pl.BlockSpec(lam_vecs.shape, lambda b, i: (0, 0)),
            pl.BlockSpec((1, LANES), lambda b, i: (0, 0)),
            pl.BlockSpec((TM, d), lambda b, i: (b * tiles + i, 0)),
            pl.BlockSpec((n_ctx, d), lambda b, i: (b * (t // n_ctx), 0)),
            pl.BlockSpec((n_ctx, d), lambda b, i: (b * (t // n_ctx), 0)),
        ],
        out_specs=pl.BlockSpec((TM, d), lambda b, i: (b * ctx_tiles + i, 0)),
        out_shape=jax.ShapeDtypeStruct((batch * n_ctx, d), BF16),
        scratch_shapes=[pltpu.VMEM((heads, n_ctx, 2 * LANES), BF16)],
        compiler_params=_cparams(("arbitrary", "arbitrary")),
        name="diff_attention_ctx",
    )(lam_vecs, subln_g, q, k, v)
    o_lat = call("diff_attention", ATTN_Q_TILES,
                 lambda b, h, i, s: (b * tiles + ctx_tiles + ATTN_Q_TILES * i + s, h),
                 t, lambda b, h, i: (b, h),
                 batch * (t - n_ctx), ATTN_Q_TILES * TM, lambda b, h, i: (b * q_steps + i, h), q_steps)
    return o_ctx, o_lat


def _gqa_attn_kernel(*refs, n_q, group):
    q_refs = refs[:n_q]
    k_ref, v_ref, o_ref, va_ref = refs[n_q:]

    @pl.when(pl.program_id(2) == 0)
    def _():
        _fill_values(v_ref, va_ref)

    n_keys = k_ref.shape[0]
    for sub, q_ref in enumerate(q_refs):
        rows = q_ref.shape[0]
        q_all = jnp.concatenate([q_ref[:, g * LANES:(g + 1) * LANES] for g in range(group)], axis=0)
        o_all = _flash(q_all, k_ref, va_ref, n_keys)
        for g in range(group):
            o_ref[sub * rows:(sub + 1) * rows, g * LANES:(g + 1) * LANES] = (
                o_all[g * rows:(g + 1) * rows].astype(o_ref.dtype))


def _gqa_attention(q, k, v, batch, t, n_ctx):
    rows, d = q.shape
    group = d // GQA_HEAD_DIM // GQA_KV_HEADS
    tiles = t // TM
    ctx_tiles = n_ctx // TM
    q_tiles = tiles - ctx_tiles
    gw = group * LANES
    n_q = GQA_Q_TILES
    assert q_tiles % n_q == 0
    q_steps = q_tiles // n_q
    q_tile_of = lambda b, h, i, s: (b * tiles + ctx_tiles + n_q * i + s, h)
    return pl.pallas_call(
        functools.partial(_gqa_attn_kernel, n_q=n_q, group=group),
        grid=(batch, GQA_KV_HEADS, q_steps),
        in_specs=[pl.BlockSpec((TM, gw), functools.partial(q_tile_of, s=s)) for s in range(n_q)] + [
            pl.BlockSpec((t, LANES), lambda b, h, i: (b, h)),
            pl.BlockSpec((t, LANES), lambda b, h, i: (b, h)),
        ],
        out_specs=pl.BlockSpec((n_q * TM, gw), lambda b, h, i: (b * q_steps + i, h)),
        out_shape=jax.ShapeDtypeStruct((batch * q_tiles * TM, d), BF16),
        scratch_shapes=[pltpu.VMEM((t, 2 * LANES), BF16)],
        compiler_params=_cparams(("arbitrary", "arbitrary", "arbitrary")),
        name="gqa_attention",
    )(*([q] * n_q), k, v)


def _layer_norm(y, g, b):
    mu = jnp.mean(y, axis=-1, keepdims=True)
    yc = y - mu
    var = jnp.mean(yc * yc, axis=-1, keepdims=True)
    return yc * lax.rsqrt(var + LN_EPS) * g + b


def _proj_route_kernel(*refs, d, tiles_per_sample, ctx_tiles, with_ctx, n_sub):
    n_parts = 2 if with_ctx else 1
    per = 2 * n_parts + 1
    shared = refs[n_sub * per:n_sub * per + 6]
    outs = refs[n_sub * per + 6:]
    for sub in range(n_sub):
        tile_refs = refs[sub * per:(sub + 1) * per]
        tile = pl.program_id(0) * n_sub + sub
        o = _merge_parts(tile_refs[:n_parts], tile, tiles_per_sample, ctx_tiles)
        x = _merge_parts(tile_refs[n_parts:2 * n_parts], tile, tiles_per_sample, ctx_tiles)
        _proj_route_tile(o, x, tile_refs[-1], *shared, *outs, sub=sub, d=d)


def _proj_route_tile(o, x, mod_ref, wo_ref, lng_ref, lnb_ref, wrh_ref, wrl_ref, br_ref,
                     x1_ref, h2_ref, route_ref, cnt_ref, *, sub, d):
    rows = slice(sub * TM, (sub + 1) * TM)
    gate_m = mod_ref[:, 2 * d:3 * d]
    shift_f = mod_ref[:, 3 * d:4 * d]
    scale_f = mod_ref[:, 4 * d:5 * d]
    ox = jnp.dot(o, wo_ref[...], preferred_element_type=F32)
    x1 = _layer_norm(DEEPNORM_ALPHA * x + gate_m * ox, lng_ref[...], lnb_ref[...])
    x1_ref[rows, :] = x1
    h2 = x1 * (1.0 + scale_f) + shift_f
    h2_ref[rows, :] = h2.astype(h2_ref.dtype)

    hh = h2.astype(BF16)
    hl = (h2 - hh.astype(F32)).astype(BF16)
    logits = jnp.dot(hh, wrh_ref[...], preferred_element_type=F32)
    logits += jnp.dot(hl, wrh_ref[...], preferred_element_type=F32)
    logits += jnp.dot(hh, wrl_ref[...], preferred_element_type=F32)
    logits += br_ref[...]

    lane = lax.broadcasted_iota(jnp.int32, logits.shape, 1).astype(F32)
    neg = jnp.full_like(logits, -jnp.inf)
    big = jnp.full_like(logits, 1e9)
    is_group = (lane >= GROUP_LANE0) & (lane < GROUP_LANE0 + N_GROUPS)
    lg = jnp.where(is_group, logits, neg)
    g_max = jnp.max(lg, axis=-1, keepdims=True)
    g_idx = jnp.min(jnp.where(lg == g_max, lane - GROUP_LANE0, big), axis=-1, keepdims=True)
    g_top = 1.0 / jnp.sum(jnp.exp(lg - g_max), axis=-1, keepdims=True)

    lane_group = jnp.floor(lane * (1.0 / EXPERTS_PER_GROUP))
    in_group = (lane < N_EXPERTS) & (lane_group == g_idx)
    le = jnp.where(in_group, logits, neg)
    m1 = jnp.max(le, axis=-1, keepdims=True)
    i1 = jnp.min(jnp.where(le == m1, lane, big), axis=-1, keepdims=True)
    le2 = jnp.where(lane == i1, neg, le)
    m2 = jnp.max(le2, axis=-1, keepdims=True)
    i2 = jnp.min(jnp.where(le2 == m2, lane, big), axis=-1, keepdims=True)
    r = jnp.exp(m2 - m1)
    w1 = g_top / (1.0 + r)
    w2 = g_top * r / (1.0 + r)

    a1 = (lane == i1).astype(F32)
    a2 = (lane == i2).astype(F32)
    both = (a1 + a2).astype(BF16)
    tm = logits.shape[0]
    rr = lax.broadcasted_iota(jnp.int32, (tm, tm), 0)
    cc = lax.broadcasted_iota(jnp.int32, (tm, tm), 1)
    strict_lower = (rr > cc).astype(BF16)
    before = jnp.dot(strict_lower, both, preferred_element_type=F32)
    rank1 = jnp.sum(a1 * before, axis=-1, keepdims=True)
    rank2 = jnp.sum(a2 * before, axis=-1, keepdims=True)
    cnt_ref[sub] = jnp.sum(a1 + a2, axis=0, keepdims=True)

    out = jnp.zeros_like(logits)
    for idx, val in ((ROUTE_E, i1), (ROUTE_E + 1, i2), (ROUTE_W, w1), (ROUTE_W + 1, w2),
                     (ROUTE_RANK, rank1), (ROUTE_RANK + 1, rank2)):
        out = jnp.where(lane == float(idx), val, out)
    route_ref[rows, :] = out


def _proj_route(o_parts, x_parts, mod, wo, ln_g, ln_b, wr_hi, wr_lo, br, tiles_per_sample, ctx_tiles, ctx_row):
    d = x_parts[0].shape[1]
    q_tiles = tiles_per_sample - ctx_tiles
    with_ctx = len(o_parts) == 2
    assert len(x_parts) == len(o_parts)
    n_sub = ROUTE_SUB_TILES
    row_map = _mod_row_map(tiles_per_sample, ctx_tiles, ctx_row)
    if with_ctx:
        n_tiles = sum(p.shape[0] for p in x_parts) // TM
        tile_of = lambda i: i
    else:
        n_tiles = o_parts[0].shape[0] // TM
        tile_of = lambda i: (i // q_tiles) * tiles_per_sample + ctx_tiles + i % q_tiles
    o_maps = _part_maps(len(o_parts), tiles_per_sample, ctx_tiles, lambda i: i)
    x_maps = _part_maps(len(x_parts), tiles_per_sample, ctx_tiles, tile_of)
    assert n_tiles % n_sub == 0
    tile_specs, tile_args = [], []
    for sub in range(n_sub):
        at = lambda fn, sub=sub: (lambda step: fn(step * n_sub + sub))
        tile_specs += [pl.BlockSpec((TM, d), at(m)) for m in o_maps + x_maps]
        tile_specs += [pl.BlockSpec((None, 1, mod.shape[-1]), at(lambda i: row_map(tile_of(i))))]
        tile_args += [*o_parts, *x_parts, mod]
    const = lambda i: (0, 0)
    return pl.pallas_call(
        functools.partial(_proj_route_kernel, d=d, tiles_per_sample=tiles_per_sample, ctx_tiles=ctx_tiles,
                          with_ctx=with_ctx, n_sub=n_sub),
        grid=(n_tiles // n_sub,),
        in_specs=tile_specs + [
            pl.BlockSpec((d, d), const),
            pl.BlockSpec((1, d), const),
            pl.BlockSpec((1, d), const),
            pl.BlockSpec((d, LANES), const),
            pl.BlockSpec((d, LANES), const),
            pl.BlockSpec((1, LANES), const),
        ],
        out_specs=[
            pl.BlockSpec((n_sub * TM, d), lambda i: (i, 0)),
            pl.BlockSpec((n_sub * TM, d), lambda i: (i, 0)),
            pl.BlockSpec((n_sub * TM, LANES), lambda i: (i, 0)),
            pl.BlockSpec((n_sub, 1, LANES), lambda i: (i, 0, 0)),
        ],
        out_shape=[
            jax.ShapeDtypeStruct((n_tiles * TM, d), F32),
            jax.ShapeDtypeStruct((n_tiles * TM, d), BF16),
            jax.ShapeDtypeStruct((n_tiles * TM, LANES), F32),
            jax.ShapeDtypeStruct((n_tiles, 1, LANES), F32),
        ],
        compiler_params=_cparams(("arbitrary",)),
        name="proj_route",
    )(*tile_args, wo, ln_g, ln_b, wr_hi, wr_lo, br)


def _dispatch_plan(counts, max_tiles, expert_base):
    n = counts[:, 0, :N_EXPERTS].astype(jnp.int32)
    units = (n + CHUNK - 1) // CHUNK
    local_off = jnp.cumsum(units, axis=1) - units
    total = jnp.sum(units, axis=0)
    tiles_e = (total + UNITS_PER_TILE - 1) // UNITS_PER_TILE
    tile_end = jnp.cumsum(tiles_e)
    region_off = (tile_end - tiles_e) * UNITS_PER_TILE
    base = region_off[None, :] + jnp.cumsum(units, axis=0) - units
    n_tiles = tile_end[-1:]
    tile_ids = jnp.arange(max_tiles, dtype=jnp.int32)
    tile_expert = jnp.sum((tile_end[None, :] <= tile_ids[:, None]).astype(jnp.int32), axis=1)
    tile_expert = jnp.minimum(tile_expert, N_EXPERTS - 1)
    tail_units = tiles_e * UNITS_PER_TILE - total
    tail_off = region_off + total
    local_off_rows = jnp.zeros((n.shape[0], 1, LANES), F32)
    local_off_rows = local_off_rows.at[:, 0, :N_EXPERTS].set((local_off * CHUNK).astype(F32))
    unit_ids = jnp.arange(LOCAL_UNITS, dtype=jnp.int32)
    local_end = local_off + units
    expert_of = jnp.sum((local_end[:, None, :] <= unit_ids[None, :, None]).astype(jnp.int32), axis=-1)
    onehot = (expert_of[:, :, None] == jnp.arange(N_EXPERTS, dtype=jnp.int32)).astype(jnp.int32)
    global_unit = jnp.sum(onehot * (base - local_off)[:, None, :], axis=-1) + unit_ids[None, :]
    i32 = lambda a: a.reshape(-1).astype(jnp.int32)
    return dict(tile_units=i32(jnp.sum(units, axis=1)), global_unit=i32(global_unit),
                tail_units=i32(tail_units), tail_off=i32(tail_off), n_tiles=i32(n_tiles),
                tile_expert=i32(tile_expert) + expert_base, local_off_rows=local_off_rows)


def _for_each_unit(tile_units_ref, global_unit_ref, tile, fn):
    def body(j, carry):
        g = global_unit_ref[tile * LOCAL_UNITS + j]
        fn(pl.multiple_of(j * CHUNK, CHUNK), pl.multiple_of(g * CHUNK, CHUNK))
        return carry
    lax.fori_loop(0, tile_units_ref[tile], body, 0)


def _local_positions(route, local_off_rows):
    lane = lax.broadcasted_iota(jnp.int32, route.shape, 1).astype(F32)
    pos = []
    for k in range(2):
        onehot = (lane == route[:, ROUTE_E + k:ROUTE_E + k + 1]).astype(F32)
        off = jnp.sum(onehot * local_off_rows, axis=-1, keepdims=True)
        pos.append(off + route[:, ROUTE_RANK + k:ROUTE_RANK + k + 1])
    return pos


def _selection(pos):
    slot = lax.broadcasted_iota(jnp.int32, (pos.shape[0], LOCAL_ROWS), 1).astype(F32)
    return slot == pos


def _split3(w):
    hi = w.astype(BF16)
    r1 = w - hi.astype(F32)
    mid = r1.astype(BF16)
    lo = (r1 - mid.astype(F32)).astype(BF16)
    return hi, mid, lo


def _dispatch_kernel(tile_units_ref, global_unit_ref, tailn_ref, tailoff_ref, nt_ref,
                     h2_ref, route_ref, loff_ref, xs_hbm, buf, zbuf, sems, zsem, *, d, n_tok_tiles, max_tiles):
    t = pl.program_id(0)
    slot = t % 2
    route = route_ref[...]
    pos1, pos2 = _local_positions(route, loff_ref[...])
    sel1 = _selection(pos1)
    sel2 = _selection(pos2)
    contract0 = (((0,), (0,)), ((), ()))
    sel = (sel1 | sel2).astype(BF16)
    buf[slot, :, 0:d] = lax.dot_general(sel, h2_ref[...], contract0, preferred_element_type=F32)

    lane = lax.broadcasted_iota(jnp.int32, route.shape, 1)
    gate_rows = jnp.zeros((LOCAL_ROWS, LANES), F32)
    for k, selk in ((0, sel1), (1, sel2)):
        pieces = _split3(route[:, ROUTE_W + k:ROUTE_W + k + 1])
        wp = jnp.zeros(route.shape, F32)
        for j, piece in enumerate(pieces):
            wp = jnp.where(lane == j, piece.astype(F32), wp)
        gate_rows += lax.dot_general(selk.astype(BF16), wp.astype(BF16), contract0,
                                     preferred_element_type=F32)
    buf[slot, :, d:] = gate_rows

    def push(s, wait):
        def fn(local_row, global_row):
            cp = pltpu.make_async_copy(buf.at[s, pl.ds(local_row, CHUNK), :],
                                       xs_hbm.at[pl.ds(global_row, CHUNK), :], sems.at[s])
            cp.wait() if wait else cp.start()
        return fn

    _for_each_unit(tile_units_ref, global_unit_ref, t, push(slot, False))

    @pl.when(t >= 1)
    def _():
        _for_each_unit(tile_units_ref, global_unit_ref, t - 1, push(1 - slot, True))

    @pl.when(t == n_tok_tiles - 1)
    def _():
        _for_each_unit(tile_units_ref, global_unit_ref, t, push(slot, True))
        zbuf[...] = jnp.zeros_like(zbuf)

        def tail_copy(e, bit):
            n = tailn_ref[e]
            done = (n >> (bit + 1)) << (bit + 1)
            row = pl.multiple_of((tailoff_ref[e] + done) * CHUNK, CHUNK)
            rows = (1 << bit) * CHUNK
            return pltpu.make_async_copy(zbuf.at[pl.ds(0, rows), :], xs_hbm.at[pl.ds(row, rows), :], zsem)

        def tile_copy(i):
            row = pl.multiple_of(i * TE, TE)
            return pltpu.make_async_copy(zbuf, xs_hbm.at[pl.ds(row, TE), :], zsem)

        for wait in (False, True):
            def tails(e, carry, wait=wait):
                for bit in reversed(range(TAIL_BITS)):
                    @pl.when(((tailn_ref[e] >> bit) & 1) == 1)
                    def _():
                        cp = tail_copy(e, bit)
                        cp.wait() if wait else cp.start()
                return carry
            lax.fori_loop(0, N_EXPERTS, tails, 0)

            def unused(i, carry, wait=wait):
                cp = tile_copy(i)
                cp.wait() if wait else cp.start()
                return carry
            lax.fori_loop(nt_ref[0], max_tiles, unused, 0)


def _dispatch(h2, route, plan, max_tiles):
    rows, d = h2.shape
    n_tok_tiles = rows // TM
    width = d + LANES
    return pl.pallas_call(
        functools.partial(_dispatch_kernel, d=d, n_tok_tiles=n_tok_tiles, max_tiles=max_tiles),
        grid_spec=pltpu.PrefetchScalarGridSpec(
            num_scalar_prefetch=5,
            grid=(n_tok_tiles,),
            in_specs=[
                pl.BlockSpec((TM, d), lambda t, *_: (t, 0)),
                pl.BlockSpec((TM, LANES), lambda t, *_: (t, 0)),
                pl.BlockSpec((None, 1, LANES), lambda t, *_: (t, 0, 0)),
            ],
            out_specs=pl.BlockSpec(memory_space=pl.ANY),
            scratch_shapes=[
                pltpu.VMEM((2, LOCAL_ROWS, width), F32),
                pltpu.VMEM((TE, width), F32),
                pltpu.SemaphoreType.DMA((2,)),
                pltpu.SemaphoreType.DMA(()),
            ],
        ),
        out_shape=jax.ShapeDtypeStruct((max_tiles * TE, width), F32),
        compiler_params=_cparams(("arbitrary",)),
        name="dispatch",
    )(plan["tile_units"], plan["global_unit"], plan["tail_units"], plan["tail_off"], plan["n_tiles"],
      h2, route, plan["local_off_rows"])


def _expert_kernel(te_ref, nt_ref, xs_ref, wg_ref, wu_ref, wd_ref, y_ref, wgu_b, wd_b, *, d, hidden):
    t = pl.program_id(0)
    nt = nt_ref[0]

    @pl.when(t < nt)
    def _():
        prev = te_ref[jnp.maximum(t - 1, 0)]

        @pl.when((t == 0) | (te_ref[t] != prev))
        def _():
            wgu_b[:, 0:hidden] = wg_ref[...].astype(BF16)
            wgu_b[:, hidden:2 * hidden] = wu_ref[...].astype(BF16)
            wd_b[...] = wd_ref[...].astype(BF16)

        x = xs_ref[:, 0:d].astype(BF16)
        gate = xs_ref[:, d:d + 1] + xs_ref[:, d + 1:d + 2] + xs_ref[:, d + 2:d + 3]
        au = jnp.dot(x, wgu_b[...], preferred_element_type=F32)
        a = au[:, 0:hidden]
        u = au[:, hidden:2 * hidden]
        act = (a * jax.nn.sigmoid(a) * u * gate).astype(BF16)
        y_ref[...] = jnp.dot(act, wd_b[...], preferred_element_type=F32)

    @pl.when(t >= nt)
    def _():
        y_ref[...] = jnp.zeros_like(y_ref)


def _experts(xs, plan, w_gate, w_up, w_down, max_tiles):
    width = xs.shape[1]
    d = width - LANES
    hidden = w_gate.shape[-1]
    last = lambda t, te, nt: jnp.minimum(t, nt[0] - 1)
    return pl.pallas_call(
        functools.partial(_expert_kernel, d=d, hidden=hidden),
        grid_spec=pltpu.PrefetchScalarGridSpec(
            num_scalar_prefetch=2,
            grid=(max_tiles,),
            in_specs=[
                pl.BlockSpec((TE, width), lambda t, te, nt: (last(t, te, nt), 0)),
                pl.BlockSpec((None, d, hidden), lambda t, te, nt: (te[last(t, te, nt)], 0, 0)),
                pl.BlockSpec((None, d, hidden), lambda t, te, nt: (te[last(t, te, nt)], 0, 0)),
                pl.BlockSpec((None, hidden, d), lambda t, te, nt: (te[last(t, te, nt)], 0, 0)),
            ],
            out_specs=pl.BlockSpec((TE, d), lambda t, te, nt: (t, 0)),
            scratch_shapes=[
                pltpu.VMEM((d, 2 * hidden), BF16),
                pltpu.VMEM((hidden, d), BF16),
            ],
        ),
        out_shape=jax.ShapeDtypeStruct((max_tiles * TE, d), F32),
        compiler_params=_cparams(("arbitrary",)),
        name="experts",
    )(plan["tile_expert"], plan["n_tiles"], xs, w_gate, w_up, w_down)


def _combine_kernel(tile_units_ref, global_unit_ref, y_hbm, x1_ref, route_ref, loff_ref, mod_ref, lng_ref,
                    lnb_ref, o_ref, ybuf, sems, *, d, n_tok_tiles):
    t = pl.program_id(0)
    slot = t % 2

    def pull(s, wait):
        def fn(local_row, global_row):
            cp = pltpu.make_async_copy(y_hbm.at[pl.ds(global_row, CHUNK), :],
                                       ybuf.at[s, pl.ds(local_row, CHUNK), :], sems.at[s])
            cp.wait() if wait else cp.start()
        return fn

    @pl.when(t == 0)
    def _():
        ybuf[...] = jnp.zeros_like(ybuf)
        _for_each_unit(tile_units_ref, global_unit_ref, 0, pull(0, False))

    @pl.when(t + 1 < n_tok_tiles)
    def _():
        _for_each_unit(tile_units_ref, global_unit_ref, t + 1, pull(1 - slot, False))

    _for_each_unit(tile_units_ref, global_unit_ref, t, pull(slot, True))

    pos1, pos2 = _local_positions(route_ref[...], loff_ref[...])
    sel = (_selection(pos1) | _selection(pos2)).astype(BF16)
    y = ybuf[slot]
    y_hi = y.astype(BF16)
    y_lo = (y - y_hi.astype(F32)).astype(BF16)
    fx = jnp.dot(sel, y_hi, preferred_element_type=F32) + jnp.dot(sel, y_lo, preferred_element_type=F32)
    gate_f = mod_ref[:, 5 * d:6 * d]
    o_ref[...] = _layer_norm(DEEPNORM_ALPHA * x1_ref[...] + gate_f * fx, lng_ref[...], lnb_ref[...])


def _combine(ys, plan, x1, route, mod, ln_g, ln_b, mod_tile_of):
    rows, d = x1.shape
    n_tok_tiles = rows // TM
    const = lambda t, *_: (0, 0)
    return pl.pallas_call(
        functools.partial(_combine_kernel, d=d, n_tok_tiles=n_tok_tiles),
        grid_spec=pltpu.PrefetchScalarGridSpec(
            num_scalar_prefetch=2,
            grid=(n_tok_tiles,),
            in_specs=[
                pl.BlockSpec(memory_space=pl.ANY),
                pl.BlockSpec((TM, d), lambda t, *_: (t, 0)),
                pl.BlockSpec((TM, LANES), lambda t, *_: (t, 0)),
                pl.BlockSpec((None, 1, LANES), lambda t, *_: (t, 0, 0)),
                pl.BlockSpec((None, 1, mod.shape[-1]), lambda t, *_: mod_tile_of(t)),
                pl.BlockSpec((1, d), const),
                pl.BlockSpec((1, d), const),
            ],
            out_specs=pl.BlockSpec((TM, d), lambda t, *_: (t, 0)),
            scratch_shapes=[
                pltpu.VMEM((2, LOCAL_ROWS, d), F32),
                pltpu.SemaphoreType.DMA((2,)),
            ],
        ),
        out_shape=jax.ShapeDtypeStruct((rows, d), F32),
        compiler_params=_cparams(("arbitrary",)),
        name="combine",
    )(plan["tile_units"], plan["global_unit"], ys, x1, route, plan["local_off_rows"], mod, ln_g, ln_b)


def kernel(x, c, ctx, c_ctx, w_mod, b_mod, ln_mix_g, ln_mix_b, ln_ffn_g, ln_ffn_b, diff_w_qkv, diff_w_o, diff_lambda_q1, diff_lambda_k1, diff_lambda_q2, diff_lambda_k2, diff_subln_g, gqa_w_qkv, gqa_w_o, gqa_q_norm_g, gqa_k_norm_g, moe_w_group, moe_b_group, moe_w_router, moe_b_router, moe_w_gate, moe_w_up, moe_w_down):
    batch, n, d = x.shape
    n_ctx = ctx.shape[1]
    t = n_ctx + n
    assert n % TM == 0 and n_ctx % TM == 0 and n % GRID_W == 0 and d % LANES == 0
    assert w_mod.shape[0] == DEPTH
    tiles_per_sample = t // TM
    ctx_tiles = n_ctx // TM
    q_tiles = tiles_per_sample - ctx_tiles

    pad = (-(batch + 1)) % SUBLANES
    cond = jnp.concatenate([c, c_ctx[None, :], jnp.zeros((pad, d), F32)], axis=0)
    ctx_row = batch
    mod_all = _modulation(cond, w_mod, b_mod)
    mod_all = mod_all.reshape(DEPTH, cond.shape[0], 1, N_MOD * d)

    x_parts = (ctx.reshape(batch * n_ctx, d), x.reshape(batch * n, d))

    for i in range(DEPTH):
        last = i == DEPTH - 1
        mod = mod_all[i]
        j = i // 2
        lng_m, lnb_m = ln_mix_g[i][None, :], ln_mix_b[i][None, :]
        lng_f, lnb_f = ln_ffn_g[i][None, :], ln_ffn_b[i][None, :]
        if i % 2 == 0:
            lambda_init = 0.8 - 0.6 * math.exp(-0.3 * i)
            tab = _rope_tables(n_ctx, n, DIFF_HEAD_DIM)
            q, k, v = _qkv_proj("diff", x_parts, mod, diff_w_qkv[j].astype(BF16), tab, None,
                                tiles_per_sample, ctx_tiles, ctx_row)
            lam_vecs = jnp.stack([diff_lambda_q1[j], diff_lambda_k1[j], diff_lambda_q2[j], diff_lambda_k2[j]])
            o_parts = _diff_attention(q, k, v, lam_vecs.astype(F32), diff_subln_g[j][None, :], batch, t, n_ctx,
                                      lambda_init)
            if last:
                o_parts = o_parts[1:]
            wo = diff_w_o[j]
        else:
            tab = _rope_tables(n_ctx, n, GQA_HEAD_DIM)
            q, k, v = _qkv_proj("gqa", x_parts, mod, gqa_w_qkv[j].astype(BF16), tab,
                                (gqa_q_norm_g[j][None, :], gqa_k_norm_g[j][None, :]),
                                tiles_per_sample, ctx_tiles, ctx_row)
            if not last:
                raise NotImplementedError("grouped-query layer with context outputs")
            o_parts = (_gqa_attention(q, k, v, batch, t, n_ctx),)
            wo = gqa_w_o[j]

        w_r = jnp.zeros((d, LANES), F32)
        w_r = w_r.at[:, :N_EXPERTS].set(moe_w_router[i]).at[:, GROUP_LANE0:GROUP_LANE0 + N_GROUPS].set(moe_w_group[i])
        b_r = jnp.zeros((1, LANES), F32)
        b_r = b_r.at[0, :N_EXPERTS].set(moe_b_router[i]).at[0, GROUP_LANE0:GROUP_LANE0 + N_GROUPS].set(moe_b_group[i])
        wr_hi = w_r.astype(BF16)
        wr_lo = (w_r - wr_hi.astype(F32)).astype(BF16)

        x1, h2, route, counts = _proj_route(o_parts, x_parts, mod, wo.astype(BF16), lng_m, lnb_m, wr_hi, wr_lo,
                                            b_r, tiles_per_sample, ctx_tiles, ctx_row)
        n_tok_tiles = x1.shape[0] // TM
        max_tiles = (2 * n_tok_tiles * TM + n_tok_tiles * N_EXPERTS * (CHUNK - 1)) // TE + N_EXPERTS
        plan = _dispatch_plan(counts, max_tiles, expert_base=i * N_EXPERTS)
        xsorted = _dispatch(h2, route, plan, max_tiles)
        hidden = moe_w_gate.shape[-1]
        w_gate = moe_w_gate.reshape(DEPTH * N_EXPERTS, d, hidden)
        w_up = moe_w_up.reshape(DEPTH * N_EXPERTS, d, hidden)
        w_down = moe_w_down.reshape(DEPTH * N_EXPERTS, hidden, d)
        ys = _experts(xsorted, plan, w_gate, w_up, w_down, max_tiles)

        row_map = _mod_row_map(tiles_per_sample, ctx_tiles, ctx_row)
        if last:
            mod_tile_of = lambda tt: row_map((tt // q_tiles) * tiles_per_sample + ctx_tiles + tt % q_tiles)
        else:
            mod_tile_of = row_map
        x_parts = (_combine(ys, plan, x1, route, mod, lng_f, lnb_f, mod_tile_of),)

    return x_parts[0].reshape(batch, n, d)
```

```python
import functools
import math

import jax
import jax.numpy as jnp
from jax import lax
from jax.experimental import pallas as pl
from jax.experimental.pallas import tpu as pltpu

F32 = jnp.float32
BF16 = jnp.bfloat16

GRID_W = 64
DIFF_HEAD_DIM = 64
GQA_HEAD_DIM = 128
GQA_KV_HEADS = 2
ROPE_THETA = 10000.0
N_GROUPS = 4
EXPERTS_PER_GROUP = 8
N_EXPERTS = N_GROUPS * EXPERTS_PER_GROUP
N_MOD = 6
LN_EPS = 1e-5
RMS_EPS = 1e-6
DEPTH = 2
DEEPNORM_ALPHA = (2 * DEPTH) ** 0.25
LOG2E = 1.4426950408889634

LANES = 128
SUBLANES = 8
TM = 256
TE = 512
ATTN_KC = 1280
ATTN_Q_TILES = 8
ATTN_SUB_TILES = 2
GQA_Q_TILES = 4
ROUTE_SUB_TILES = 2
MOD_COLS = 512
VMEM_LIMIT = 48 * 1024 * 1024

GROUP_LANE0 = N_EXPERTS

CHUNK = SUBLANES
UNITS_PER_TILE = TE // CHUNK
TAIL_BITS = (UNITS_PER_TILE - 1).bit_length()
LOCAL_ROWS = 2 * TM + N_EXPERTS * CHUNK
LOCAL_UNITS = LOCAL_ROWS // CHUNK
ROUTE_E, ROUTE_W, ROUTE_RANK = 0, 2, 4


def _cparams(sem):
    return pltpu.CompilerParams(dimension_semantics=sem, vmem_limit_bytes=VMEM_LIMIT)


def _mod_kernel(c_ref, w_ref, b_ref, o_ref):
    c = c_ref[...]
    s = c * jax.nn.sigmoid(c)
    w = w_ref[...]
    sh = s.astype(BF16)
    sl = (s - sh.astype(F32)).astype(BF16)
    wh = w.astype(BF16)
    wl = (w - wh.astype(F32)).astype(BF16)
    acc = jnp.dot(sh, wh, preferred_element_type=F32)
    acc += jnp.dot(sl, wh, preferred_element_type=F32)
    acc += jnp.dot(sh, wl, preferred_element_type=F32)
    o_ref[...] = acc + b_ref[...]


def _modulation(cond, w_mod, b_mod):
    depth, d, width = w_mod.shape
    r = cond.shape[0]
    tn = MOD_COLS
    return pl.pallas_call(
        _mod_kernel,
        grid=(depth, width // tn),
        in_specs=[
            pl.BlockSpec((r, d), lambda i, j: (0, 0)),
            pl.BlockSpec((None, d, tn), lambda i, j: (i, 0, j)),
            pl.BlockSpec((None, 1, tn), lambda i, j: (i, 0, j)),
        ],
        out_specs=pl.BlockSpec((None, r, tn), lambda i, j: (i, 0, j)),
        out_shape=jax.ShapeDtypeStruct((depth, r, width), F32),
        compiler_params=_cparams(("arbitrary", "arbitrary")),
        name="modulation",
    )(cond, w_mod, b_mod.reshape(depth, 1, width))


def _rope_tables(n_ctx, n, head_dim):
    rows = n // GRID_W
    row = jnp.broadcast_to(jnp.arange(rows, dtype=F32)[:, None], (rows, GRID_W)).reshape(-1)
    col = jnp.broadcast_to(jnp.arange(GRID_W, dtype=F32)[None, :], (rows, GRID_W)).reshape(-1)
    axis_dim = head_dim // 2
    inv_freq = ROPE_THETA ** (-jnp.arange(0, axis_dim, 2, dtype=F32) / axis_dim)
    ang = jnp.stack([row, col], axis=-1)[:, :, None] * inv_freq
    cos, sin = jnp.cos(ang), jnp.sin(ang)
    zero = jnp.zeros_like(sin)
    c = jnp.concatenate([cos, cos], axis=-1).reshape(n, head_dim)
    s_first = jnp.concatenate([-sin, zero], axis=-1).reshape(n, head_dim)
    s_second = jnp.concatenate([zero, sin], axis=-1).reshape(n, head_dim)
    tab = jnp.stack([c, s_first, s_second])
    tab = jnp.tile(tab, (1, 1, LANES // head_dim))
    ident = jnp.stack([jnp.ones((n_ctx, LANES), F32), jnp.zeros((n_ctx, LANES), F32),
                       jnp.zeros((n_ctx, LANES), F32)])
    return jnp.concatenate([ident, tab], axis=1)


def _rope_chunk(x, c, s_first, s_second, quarter):
    return x * c + pltpu.roll(x, LANES - quarter, 1) * s_first + pltpu.roll(x, quarter, 1) * s_second


def _qkv_diff_body(x, mod_ref, w_ref, tab_ref, q_ref, k_ref, v_ref, *, d, qscale):
    shift = mod_ref[:, 0:d]
    scale = mod_ref[:, d:2 * d]
    h = (x * (1.0 + scale) + shift).astype(BF16)
    qkv = jnp.dot(h, w_ref[...], preferred_element_type=F32)
    c, s_first, s_second = tab_ref[0], tab_ref[1], tab_ref[2]
    quarter = DIFF_HEAD_DIM // 4
    for j in range(d // LANES):
        lo, hi = j * LANES, (j + 1) * LANES
        q = _rope_chunk(qkv[:, lo:hi], c, s_first, s_second, quarter)
        q_ref[:, lo:hi] = (q * qscale).astype(BF16)
        k = _rope_chunk(qkv[:, d + lo:d + hi], c, s_first, s_second, quarter)
        k_ref[:, lo:hi] = k.astype(BF16)
    v_ref[...] = qkv[:, 2 * d:].astype(BF16)


def _rms_head(x, g):
    return x * lax.rsqrt(jnp.mean(x * x, axis=-1, keepdims=True) + RMS_EPS) * g


def _qkv_gqa_body(x, mod_ref, w_ref, tab_ref, qg_ref, kg_ref, q_ref, k_ref, v_ref, *, d, qscale):
    shift = mod_ref[:, 0:d]
    scale = mod_ref[:, d:2 * d]
    h = (x * (1.0 + scale) + shift).astype(BF16)
    qkv = jnp.dot(h, w_ref[...], preferred_element_type=F32)
    c, s_first, s_second = tab_ref[0], tab_ref[1], tab_ref[2]
    quarter = GQA_HEAD_DIM // 4
    kv_w = GQA_KV_HEADS * GQA_HEAD_DIM
    for j in range(d // LANES):
        lo, hi = j * LANES, (j + 1) * LANES
        q = _rope_chunk(_rms_head(qkv[:, lo:hi], qg_ref[...]), c, s_first, s_second, quarter)
        q_ref[:, lo:hi] = (q * qscale).astype(BF16)
    for j in range(GQA_KV_HEADS):
        lo, hi = j * LANES, (j + 1) * LANES
        k = _rope_chunk(_rms_head(qkv[:, d + lo:d + hi], kg_ref[...]), c, s_first, s_second, quarter)
        k_ref[:, lo:hi] = k.astype(BF16)
    v_ref[...] = qkv[:, d + kv_w:].astype(BF16)


def _mod_row_map(tiles_per_sample, ctx_tiles, ctx_row):
    def index(i):
        b = i // tiles_per_sample
        j = i % tiles_per_sample
        return (jnp.where(j < ctx_tiles, ctx_row, b), 0, 0)
    return index


def _merge_parts(part_refs, tile, tiles_per_sample, ctx_tiles):
    if len(part_refs) == 1:
        return part_refs[0][...]
    ctx_ref, lat_ref = part_refs
    tile_in_sample = jnp.full(lat_ref.shape, tile % tiles_per_sample, jnp.int32)
    return jnp.where(tile_in_sample < ctx_tiles, ctx_ref[...], lat_ref[...])


def _part_maps(n_parts, tiles_per_sample, ctx_tiles, tile_of):
    if n_parts == 1:
        return [lambda i: (tile_of(i), 0)]
    q_tiles = tiles_per_sample - ctx_tiles
    return [
        lambda i: ((i // tiles_per_sample) * ctx_tiles + jnp.minimum(i % tiles_per_sample, ctx_tiles - 1), 0),
        lambda i: ((i // tiles_per_sample) * q_tiles + jnp.maximum(i % tiles_per_sample - ctx_tiles, 0), 0),
    ]


def _qkv_kernel(*refs, body, n_parts, tiles_per_sample, ctx_tiles):
    x = _merge_parts(refs[:n_parts], pl.program_id(0), tiles_per_sample, ctx_tiles)
    body(x, *refs[n_parts:])


def _qkv_proj(kind, x_parts, mod, w, tab, norm_g, tiles_per_sample, ctx_tiles, ctx_row):
    d = x_parts[0].shape[1]
    rows = sum(p.shape[0] for p in x_parts)
    n_tiles = rows // TM
    width = w.shape[1]
    row_map = _mod_row_map(tiles_per_sample, ctx_tiles, ctx_row)
    in_specs = [pl.BlockSpec((TM, d), m)
                for m in _part_maps(len(x_parts), tiles_per_sample, ctx_tiles, lambda i: i)] + [
        pl.BlockSpec((None, 1, mod.shape[-1]), row_map),
        pl.BlockSpec((d, width), lambda i: (0, 0)),
        pl.BlockSpec((3, TM, LANES), lambda i: (0, i % tiles_per_sample, 0)),
    ]
    args = [*x_parts, mod, w, tab]
    if kind == "diff":
        body = functools.partial(_qkv_diff_body, d=d, qscale=DIFF_HEAD_DIM ** -0.5 * LOG2E)
        kw, vw = d, d
    else:
        body = functools.partial(_qkv_gqa_body, d=d, qscale=GQA_HEAD_DIM ** -0.5 * LOG2E)
        kw = vw = GQA_KV_HEADS * GQA_HEAD_DIM
        in_specs += [pl.BlockSpec((1, LANES), lambda i: (0, 0))] * 2
        args += list(norm_g)
    kern = functools.partial(_qkv_kernel, body=body, n_parts=len(x_parts), tiles_per_sample=tiles_per_sample,
                             ctx_tiles=ctx_tiles)
    return pl.pallas_call(
        kern,
        grid=(n_tiles,),
        in_specs=in_specs,
        out_specs=[
            pl.BlockSpec((TM, d), lambda i: (i, 0)),
            pl.BlockSpec((TM, kw), lambda i: (i, 0)),
            pl.BlockSpec((TM, vw), lambda i: (i, 0)),
        ],
        out_shape=[
            jax.ShapeDtypeStruct((rows, d), BF16),
            jax.ShapeDtypeStruct((rows, kw), BF16),
            jax.ShapeDtypeStruct((rows, vw), BF16),
        ],
        compiler_params=_cparams(("arbitrary",)),
        name="qkv_" + kind,
    )(*args)


def _flash(q, k_ref, va_ref, n_keys):
    m = None
    acc = None
    for lo in range(0, n_keys, ATTN_KC):
        hi = min(lo + ATTN_KC, n_keys)
        s = lax.dot_general(q, k_ref[lo:hi, :], (((1,), (1,)), ((), ())), preferred_element_type=F32)
        m_new = jnp.max(s, axis=-1, keepdims=True)
        if m is not None:
            m_new = jnp.maximum(m, m_new)
        p = jnp.exp2(s - m_new).astype(BF16)
        pv = jnp.dot(p, va_ref[lo:hi, :], preferred_element_type=F32)
        acc = pv if acc is None else jnp.exp2(m - m_new) * acc + pv
        m = m_new
    return acc[:, 0:LANES] / acc[:, LANES:LANES + 1]


def _fill_values(v_ref, va_ref):
    va_ref[:, 0:LANES] = v_ref[...]
    lane = lax.broadcasted_iota(jnp.int32, v_ref.shape, 1)
    va_ref[:, LANES:2 * LANES] = jnp.where(lane == 0, 1.0, 0.0).astype(va_ref.dtype)


def _diff_lambda(lam_ref, lambda_init):
    lv = lam_ref[...]
    return (jnp.exp(jnp.sum(lv[0:1] * lv[1:2], axis=-1, keepdims=True))
            - jnp.exp(jnp.sum(lv[2:3] * lv[3:4], axis=-1, keepdims=True)) + lambda_init)


def _diff_head(q, lam, g, k_ref, va_ref, lambda_init):
    rows = q.shape[0]
    lane = lax.broadcasted_iota(jnp.int32, q.shape, 1)
    zero = jnp.zeros_like(q)
    q12 = jnp.concatenate([jnp.where(lane < DIFF_HEAD_DIM, q, zero),
                           jnp.where(lane >= DIFF_HEAD_DIM, q, zero)], axis=0)
    o12 = _flash(q12, k_ref, va_ref, k_ref.shape[0])
    o = o12[0:rows] - lam * o12[rows:2 * rows]
    o = o * lax.rsqrt(jnp.mean(o * o, axis=-1, keepdims=True) + RMS_EPS) * g
    return o * (1.0 - lambda_init)


def _diff_attn_kernel(lam_ref, g_ref, *refs, n_q, lambda_init):
    q_refs = refs[:n_q]
    k_ref, v_ref, o_ref, va_ref = refs[n_q:]

    @pl.when(pl.program_id(2) == 0)
    def _():
        _fill_values(v_ref, va_ref)

    lam = _diff_lambda(lam_ref, lambda_init)
    per = min(ATTN_SUB_TILES, n_q)
    for sub in range(n_q // per):
        parts = [r[...] for r in q_refs[sub * per:(sub + 1) * per]]
        q = jnp.concatenate(parts, axis=0) if per > 1 else parts[0]
        rows = q.shape[0]
        o = _diff_head(q, lam, g_ref[...], k_ref, va_ref, lambda_init)
        o_ref[sub * rows:(sub + 1) * rows, :] = o.astype(o_ref.dtype)


def _diff_ctx_kernel(lam_ref, g_ref, q_ref, k_ref, v_ref, o_ref, va_ref, *, heads, lambda_init):
    head = lambda ref, h: ref.at[:, pl.ds(h * LANES, LANES)]

    @pl.when(pl.program_id(1) == 0)
    def _():
        for h in range(heads):
            _fill_values(head(v_ref, h), va_ref.at[h])

    lam = _diff_lambda(lam_ref, lambda_init)
    for h in range(heads):
        o = _diff_head(q_ref[:, h * LANES:(h + 1) * LANES], lam, g_ref[...], head(k_ref, h), va_ref.at[h],
                       lambda_init)
        o_ref[:, h * LANES:(h + 1) * LANES] = o.astype(o_ref.dtype)


def _diff_attention(q, k, v, lam_vecs, subln_g, batch, t, n_ctx, lambda_init):
    rows, d = q.shape
    heads = d // LANES
    tiles = t // TM
    ctx_tiles = n_ctx // TM
    q_steps = (tiles - ctx_tiles) // ATTN_Q_TILES
    assert t % n_ctx == 0 and (tiles - ctx_tiles) % ATTN_Q_TILES == 0
    const = lambda b, h, i: (0, 0)

    def call(name, n_q, q_tile_of, kv_rows, kv_block, out_rows, out_block_rows, out_index, steps):
        return pl.pallas_call(
            functools.partial(_diff_attn_kernel, n_q=n_q, lambda_init=lambda_init),
            grid=(batch, heads, steps),
            in_specs=[pl.BlockSpec(lam_vecs.shape, const), pl.BlockSpec((1, LANES), const)]
            + [pl.BlockSpec((TM, LANES), functools.partial(q_tile_of, s=s)) for s in range(n_q)]
            + [pl.BlockSpec((kv_rows, LANES), kv_block)] * 2,
            out_specs=pl.BlockSpec((out_block_rows, LANES), out_index),
            out_shape=jax.ShapeDtypeStruct((out_rows, d), BF16),
            scratch_shapes=[pltpu.VMEM((kv_rows, 2 * LANES), BF16)],
            compiler_params=_cparams(("arbitrary", "arbitrary", "arbitrary")),
            name=name,
        )(lam_vecs, subln_g, *([q] * n_q), k, v)

    o_ctx = pl.pallas_call(
        functools.partial(_diff_ctx_kernel, heads=heads, lambda_init=lambda_init),
        grid=(batch, ctx_tiles),
        in_specs=[
            pl.BlockSpec(lam_vecs.shape, lambda b, i: (0, 0)),
            pl.BlockSpec((1, LANES), lambda b, i: (0, 0)),
            pl.BlockSpec((TM, d), lambda b, i: (b * tiles + i, 0)),
            pl.BlockSpec((n_ctx, d), lambda b, i: (b * (t // n_ctx), 0)),
            pl.BlockSpec((n_ctx, d), lambda b, i: (b * (t // n_ctx), 0)),
        ],
        out_specs=pl.BlockSpec((TM, d), lambda b, i: (b * ctx_tiles + i, 0)),
        out_shape=jax.ShapeDtypeStruct((batch * n_ctx, d), BF16),
        scratch_shapes=[pltpu.VMEM((heads, n_ctx, 2 * LANES), BF16)],
        compiler_params=_cparams(("arbitrary", "arbitrary")),
        name="diff_attention_ctx",
    )(lam_vecs, subln_g, q, k, v)
    o_lat = call("diff_attention", ATTN_Q_TILES,
                 lambda b, h, i, s: (b * tiles + ctx_tiles + ATTN_Q_TILES * i + s, h),
                 t, lambda b, h, i: (b, h),
                 batch * (t - n_ctx), ATTN_Q_TILES * TM, lambda b, h, i: (b * q_steps + i, h), q_steps)
    return o_ctx, o_lat


def _gqa_attn_kernel(*refs, n_q, group):
    q_refs = refs[:n_q]
    k_ref, v_ref, o_ref, va_ref = refs[n_q:]

    @pl.when(pl.program_id(2) == 0)
    def _():
        _fill_values(v_ref, va_ref)

    n_keys = k_ref.shape[0]
    for sub, q_ref in enumerate(q_refs):
        rows = q_ref.shape[0]
        q_all = jnp.concatenate([q_ref[:, g * LANES:(g + 1) * LANES] for g in range(group)], axis=0)
        o_all = _flash(q_all, k_ref, va_ref, n_keys)
        for g in range(group):
            o_ref[sub * rows:(sub + 1) * rows, g * LANES:(g + 1) * LANES] = (
                o_all[g * rows:(g + 1) * rows].astype(o_ref.dtype))


def _gqa_attention(q, k, v, batch, t, n_ctx):
    rows, d = q.shape
    group = d // GQA_HEAD_DIM // GQA_KV_HEADS
    tiles = t // TM
    ctx_tiles = n_ctx // TM
    q_tiles = tiles - ctx_tiles
    gw = group * LANES
    n_q = GQA_Q_TILES
    assert q_tiles % n_q == 0
    q_steps = q_tiles // n_q
    q_tile_of = lambda b, h, i, s: (b * tiles + ctx_tiles + n_q * i + s, h)
    return pl.pallas_call(
        functools.partial(_gqa_attn_kernel, n_q=n_q, group=group),
        grid=(batch, GQA_KV_HEADS, q_steps),
        in_specs=[pl.BlockSpec((TM, gw), functools.partial(q_tile_of, s=s)) for s in range(n_q)] + [
            pl.BlockSpec((t, LANES), lambda b, h, i: (b, h)),
            pl.BlockSpec((t, LANES), lambda b, h, i: (b, h)),
        ],
        out_specs=pl.BlockSpec((n_q * TM, gw), lambda b, h, i: (b * q_steps + i, h)),
        out_shape=jax.ShapeDtypeStruct((batch * q_tiles * TM, d), BF16),
        scratch_shapes=[pltpu.VMEM((t, 2 * LANES), BF16)],
        compiler_params=_cparams(("arbitrary", "arbitrary", "arbitrary")),
        name="gqa_attention",
    )(*([q] * n_q), k, v)


def _layer_norm(y, g, b):
    mu = jnp.mean(y, axis=-1, keepdims=True)
    yc = y - mu
    var = jnp.mean(yc * yc, axis=-1, keepdims=True)
    return yc * lax.rsqrt(var + LN_EPS) * g + b


def _proj_route_kernel(*refs, d, tiles_per_sample, ctx_tiles, with_ctx, n_sub):
    n_parts = 2 if with_ctx else 1
    per = 2 * n_parts + 1
    shared = refs[n_sub * per:n_sub * per + 6]
    outs = refs[n_sub * per + 6:]
    for sub in range(n_sub):
        tile_refs = refs[sub * per:(sub + 1) * per]
        tile = pl.program_id(0) * n_sub + sub
        o = _merge_parts(tile_refs[:n_parts], tile, tiles_per_sample, ctx_tiles)
        x = _merge_parts(tile_refs[n_parts:2 * n_parts], tile, tiles_per_sample, ctx_tiles)
        _proj_route_tile(o, x, tile_refs[-1], *shared, *outs, sub=sub, d=d)


def _proj_route_tile(o, x, mod_ref, wo_ref, lng_ref, lnb_ref, wrh_ref, wrl_ref, br_ref,
                     x1_ref, h2_ref, route_ref, cnt_ref, *, sub, d):
    rows = slice(sub * TM, (sub + 1) * TM)
    gate_m = mod_ref[:, 2 * d:3 * d]
    shift_f = mod_ref[:, 3 * d:4 * d]
    scale_f = mod_ref[:, 4 * d:5 * d]
    ox = jnp.dot(o, wo_ref[...], preferred_element_type=F32)
    x1 = _layer_norm(DEEPNORM_ALPHA * x + gate_m * ox, lng_ref[...], lnb_ref[...])
    x1_ref[rows, :] = x1
    h2 = x1 * (1.0 + scale_f) + shift_f
    h2_ref[rows, :] = h2.astype(h2_ref.dtype)

    hh = h2.astype(BF16)
    hl = (h2 - hh.astype(F32)).astype(BF16)
    logits = jnp.dot(hh, wrh_ref[...], preferred_element_type=F32)
    logits += jnp.dot(hl, wrh_ref[...], preferred_element_type=F32)
    logits += jnp.dot(hh, wrl_ref[...], preferred_element_type=F32)
    logits += br_ref[...]

    lane = lax.broadcasted_iota(jnp.int32, logits.shape, 1).astype(F32)
    neg = jnp.full_like(logits, -jnp.inf)
    big = jnp.full_like(logits, 1e9)
    is_group = (lane >= GROUP_LANE0) & (lane < GROUP_LANE0 + N_GROUPS)
    lg = jnp.where(is_group, logits, neg)
    g_max = jnp.max(lg, axis=-1, keepdims=True)
    g_idx = jnp.min(jnp.where(lg == g_max, lane - GROUP_LANE0, big), axis=-1, keepdims=True)
    g_top = 1.0 / jnp.sum(jnp.exp(lg - g_max), axis=-1, keepdims=True)

    lane_group = jnp.floor(lane * (1.0 / EXPERTS_PER_GROUP))
    in_group = (lane < N_EXPERTS) & (lane_group == g_idx)
    le = jnp.where(in_group, logits, neg)
    m1 = jnp.max(le, axis=-1, keepdims=True)
    i1 = jnp.min(jnp.where(le == m1, lane, big), axis=-1, keepdims=True)
    le2 = jnp.where(lane == i1, neg, le)
    m2 = jnp.max(le2, axis=-1, keepdims=True)
    i2 = jnp.min(jnp.where(le2 == m2, lane, big), axis=-1, keepdims=True)
    r = jnp.exp(m2 - m1)
    w1 = g_top / (1.0 + r)
    w2 = g_top * r / (1.0 + r)

    a1 = (lane == i1).astype(F32)
    a2 = (lane == i2).astype(F32)
    both = (a1 + a2).astype(BF16)
    tm = logits.shape[0]
    rr = lax.broadcasted_iota(jnp.int32, (tm, tm), 0)
    cc = lax.broadcasted_iota(jnp.int32, (tm, tm), 1)
    strict_lower = (rr > cc).astype(BF16)
    before = jnp.dot(strict_lower, both, preferred_element_type=F32)
    rank1 = jnp.sum(a1 * before, axis=-1, keepdims=True)
    rank2 = jnp.sum(a2 * before, axis=-1, keepdims=True)
    cnt_ref[sub] = jnp.sum(a1 + a2, axis=0, keepdims=True)

    out = jnp.zeros_like(logits)
    for idx, val in ((ROUTE_E, i1), (ROUTE_E + 1, i2), (ROUTE_W, w1), (ROUTE_W + 1, w2),
                     (ROUTE_RANK, rank1), (ROUTE_RANK + 1, rank2)):
        out = jnp.where(lane == float(idx), val, out)
    route_ref[rows, :] = out


def _proj_route(o_parts, x_parts, mod, wo, ln_g, ln_b, wr_hi, wr_lo, br, tiles_per_sample, ctx_tiles, ctx_row):
    d = x_parts[0].shape[1]
    q_tiles = tiles_per_sample - ctx_tiles
    with_ctx = len(o_parts) == 2
    assert len(x_parts) == len(o_parts)
    n_sub = ROUTE_SUB_TILES
    row_map = _mod_row_map(tiles_per_sample, ctx_tiles, ctx_row)
    if with_ctx:
        n_tiles = sum(p.shape[0] for p in x_parts) // TM
        tile_of = lambda i: i
    else:
        n_tiles = o_parts[0].shape[0] // TM
        tile_of = lambda i: (i // q_tiles) * tiles_per_sample + ctx_tiles + i % q_tiles
    o_maps = _part_maps(len(o_parts), tiles_per_sample, ctx_tiles, lambda i: i)
    x_maps = _part_maps(len(x_parts), tiles_per_sample, ctx_tiles, tile_of)
    assert n_tiles % n_sub == 0
    tile_specs, tile_args = [], []
    for sub in range(n_sub):
        at = lambda fn, sub=sub: (lambda step: fn(step * n_sub + sub))
        tile_specs += [pl.BlockSpec((TM, d), at(m)) for m in o_maps + x_maps]
        tile_specs += [pl.BlockSpec((None, 1, mod.shape[-1]), at(lambda i: row_map(tile_of(i))))]
        tile_args += [*o_parts, *x_parts, mod]
    const = lambda i: (0, 0)
    return pl.pallas_call(
        functools.partial(_proj_route_kernel, d=d, tiles_per_sample=tiles_per_sample, ctx_tiles=ctx_tiles,
                          with_ctx=with_ctx, n_sub=n_sub),
        grid=(n_tiles // n_sub,),
        in_specs=tile_specs + [
            pl.BlockSpec((d, d), const),
            pl.BlockSpec((1, d), const),
            pl.BlockSpec((1, d), const),
            pl.BlockSpec((d, LANES), const),
            pl.BlockSpec((d, LANES), const),
            pl.BlockSpec((1, LANES), const),
        ],
        out_specs=[
            pl.BlockSpec((n_sub * TM, d), lambda i: (i, 0)),
            pl.BlockSpec((n_sub * TM, d), lambda i: (i, 0)),
            pl.BlockSpec((n_sub * TM, LANES), lambda i: (i, 0)),
            pl.BlockSpec((n_sub, 1, LANES), lambda i: (i, 0, 0)),
        ],
        out_shape=[
            jax.ShapeDtypeStruct((n_tiles * TM, d), F32),
            jax.ShapeDtypeStruct((n_tiles * TM, d), BF16),
            jax.ShapeDtypeStruct((n_tiles * TM, LANES), F32),
            jax.ShapeDtypeStruct((n_tiles, 1, LANES), F32),
        ],
        compiler_params=_cparams(("arbitrary",)),
        name="proj_route",
    )(*tile_args, wo, ln_g, ln_b, wr_hi, wr_lo, br)


def _dispatch_plan(counts, max_tiles, expert_base):
    n = counts[:, 0, :N_EXPERTS].astype(jnp.int32)
    units = (n + CHUNK - 1) // CHUNK
    local_off = jnp.cumsum(units, axis=1) - units
    total = jnp.sum(units, axis=0)
    tiles_e = (total + UNITS_PER_TILE - 1) // UNITS_PER_TILE
    tile_end = jnp.cumsum(tiles_e)
    region_off = (tile_end - tiles_e) * UNITS_PER_TILE
    base = region_off[None, :] + jnp.cumsum(units, axis=0) - units
    n_tiles = tile_end[-1:]
    tile_ids = jnp.arange(max_tiles, dtype=jnp.int32)
    tile_expert = jnp.sum((tile_end[None, :] <= tile_ids[:, None]).astype(jnp.int32), axis=1)
    tile_expert = jnp.minimum(tile_expert, N_EXPERTS - 1)
    tail_units = tiles_e * UNITS_PER_TILE - total
    tail_off = region_off + total
    local_off_rows = jnp.zeros((n.shape[0], 1, LANES), F32)
    local_off_rows = local_off_rows.at[:, 0, :N_EXPERTS].set((local_off * CHUNK).astype(F32))
    unit_ids = jnp.arange(LOCAL_UNITS, dtype=jnp.int32)
    local_end = local_off + units
    expert_of = jnp.sum((local_end[:, None, :] <= unit_ids[None, :, None]).astype(jnp.int32), axis=-1)
    onehot = (expert_of[:, :, None] == jnp.arange(N_EXPERTS, dtype=jnp.int32)).astype(jnp.int32)
    global_unit = jnp.sum(onehot * (base - local_off)[:, None, :], axis=-1) + unit_ids[None, :]
    i32 = lambda a: a.reshape(-1).astype(jnp.int32)
    return dict(tile_units=i32(jnp.sum(units, axis=1)), global_unit=i32(global_unit),
                tail_units=i32(tail_units), tail_off=i32(tail_off), n_tiles=i32(n_tiles),
                tile_expert=i32(tile_expert) + expert_base, local_off_rows=local_off_rows)


def _for_each_unit(tile_units_ref, global_unit_ref, tile, fn):
    def body(j, carry):
        g = global_unit_ref[tile * LOCAL_UNITS + j]
        fn(pl.multiple_of(j * CHUNK, CHUNK), pl.multiple_of(g * CHUNK, CHUNK))
        return carry
    lax.fori_loop(0, tile_units_ref[tile], body, 0)


def _local_positions(route, local_off_rows):
    lane = lax.broadcasted_iota(jnp.int32, route.shape, 1).astype(F32)
    pos = []
    for k in range(2):
        onehot = (lane == route[:, ROUTE_E + k:ROUTE_E + k + 1]).astype(F32)
        off = jnp.sum(onehot * local_off_rows, axis=-1, keepdims=True)
        pos.append(off + route[:, ROUTE_RANK + k:ROUTE_RANK + k + 1])
    return pos


def _selection(pos):
    slot = lax.broadcasted_iota(jnp.int32, (pos.shape[0], LOCAL_ROWS), 1).astype(F32)
    return slot == pos


def _split3(w):
    hi = w.astype(BF16)
    r1 = w - hi.astype(F32)
    mid = r1.astype(BF16)
    lo = (r1 - mid.astype(F32)).astype(BF16)
    return hi, mid, lo


def _dispatch_kernel(tile_units_ref, global_unit_ref, tailn_ref, tailoff_ref, nt_ref,
                     h2_ref, route_ref, loff_ref, xs_hbm, buf, zbuf, sems, zsem, *, d, n_tok_tiles, max_tiles):
    t = pl.program_id(0)
    slot = t % 2
    route = route_ref[...]
    pos1, pos2 = _local_positions(route, loff_ref[...])
    sel1 = _selection(pos1)
    sel2 = _selection(pos2)
    contract0 = (((0,), (0,)), ((), ()))
    sel = (sel1 | sel2).astype(BF16)
    buf[slot, :, 0:d] = lax.dot_general(sel, h2_ref[...], contract0, preferred_element_type=F32)

    lane = lax.broadcasted_iota(jnp.int32, route.shape, 1)
    gate_rows = jnp.zeros((LOCAL_ROWS, LANES), F32)
    for k, selk in ((0, sel1), (1, sel2)):
        pieces = _split3(route[:, ROUTE_W + k:ROUTE_W + k + 1])
        wp = jnp.zeros(route.shape, F32)
        for j, piece in enumerate(pieces):
            wp = jnp.where(lane == j, piece.astype(F32), wp)
        gate_rows += lax.dot_general(selk.astype(BF16), wp.astype(BF16), contract0,
                                     preferred_element_type=F32)
    buf[slot, :, d:] = gate_rows

    def push(s, wait):
        def fn(local_row, global_row):
            cp = pltpu.make_async_copy(buf.at[s, pl.ds(local_row, CHUNK), :],
                                       xs_hbm.at[pl.ds(global_row, CHUNK), :], sems.at[s])
            cp.wait() if wait else cp.start()
        return fn

    _for_each_unit(tile_units_ref, global_unit_ref, t, push(slot, False))

    @pl.when(t >= 1)
    def _():
        _for_each_unit(tile_units_ref, global_unit_ref, t - 1, push(1 - slot, True))

    @pl.when(t == n_tok_tiles - 1)
    def _():
        _for_each_unit(tile_units_ref, global_unit_ref, t, push(slot, True))
        zbuf[...] = jnp.zeros_like(zbuf)

        def tail_copy(e, bit):
            n = tailn_ref[e]
            done = (n >> (bit + 1)) << (bit + 1)
            row = pl.multiple_of((tailoff_ref[e] + done) * CHUNK, CHUNK)
            rows = (1 << bit) * CHUNK
            return pltpu.make_async_copy(zbuf.at[pl.ds(0, rows), :], xs_hbm.at[pl.ds(row, rows), :], zsem)

        def tile_copy(i):
            row = pl.multiple_of(i * TE, TE)
            return pltpu.make_async_copy(zbuf, xs_hbm.at[pl.ds(row, TE), :], zsem)

        for wait in (False, True):
            def tails(e, carry, wait=wait):
                for bit in reversed(range(TAIL_BITS)):
                    @pl.when(((tailn_ref[e] >> bit) & 1) == 1)
                    def _():
                        cp = tail_copy(e, bit)
                        cp.wait() if wait else cp.start()
                return carry
            lax.fori_loop(0, N_EXPERTS, tails, 0)

            def unused(i, carry, wait=wait):
                cp = tile_copy(i)
                cp.wait() if wait else cp.start()
                return carry
            lax.fori_loop(nt_ref[0], max_tiles, unused, 0)


def _dispatch(h2, route, plan, max_tiles):
    rows, d = h2.shape
    n_tok_tiles = rows // TM
    width = d + LANES
    return pl.pallas_call(
        functools.partial(_dispatch_kernel, d=d, n_tok_tiles=n_tok_tiles, max_tiles=max_tiles),
        grid_spec=pltpu.PrefetchScalarGridSpec(
            num_scalar_prefetch=5,
            grid=(n_tok_tiles,),
            in_specs=[
                pl.BlockSpec((TM, d), lambda t, *_: (t, 0)),
                pl.BlockSpec((TM, LANES), lambda t, *_: (t, 0)),
                pl.BlockSpec((None, 1, LANES), lambda t, *_: (t, 0, 0)),
            ],
            out_specs=pl.BlockSpec(memory_space=pl.ANY),
            scratch_shapes=[
                pltpu.VMEM((2, LOCAL_ROWS, width), F32),
                pltpu.VMEM((TE, width), F32),
                pltpu.SemaphoreType.DMA((2,)),
                pltpu.SemaphoreType.DMA(()),
            ],
        ),
        out_shape=jax.ShapeDtypeStruct((max_tiles * TE, width), F32),
        compiler_params=_cparams(("arbitrary",)),
        name="dispatch",
    )(plan["tile_units"], plan["global_unit"], plan["tail_units"], plan["tail_off"], plan["n_tiles"],
      h2, route, plan["local_off_rows"])


def _expert_kernel(te_ref, nt_ref, xs_ref, wg_ref, wu_ref, wd_ref, y_ref, wgu_b, wd_b, *, d, hidden):
    t = pl.program_id(0)
    nt = nt_ref[0]

    @pl.when(t < nt)
    def _():
        prev = te_ref[jnp.maximum(t - 1, 0)]

        @pl.when((t == 0) | (te_ref[t] != prev))
        def _():
            wgu_b[:, 0:hidden] = wg_ref[...].astype(BF16)
            wgu_b[:, hidden:2 * hidden] = wu_ref[...].astype(BF16)
            wd_b[...] = wd_ref[...].astype(BF16)

        x = xs_ref[:, 0:d].astype(BF16)
        gate = xs_ref[:, d:d + 1] + xs_ref[:, d + 1:d + 2] + xs_ref[:, d + 2:d + 3]
        au = jnp.dot(x, wgu_b[...], preferred_element_type=F32)
        a = au[:, 0:hidden]
        u = au[:, hidden:2 * hidden]
        act = (a * jax.nn.sigmoid(a) * u * gate).astype(BF16)
        y_ref[...] = jnp.dot(act, wd_b[...], preferred_element_type=F32)

    @pl.when(t >= nt)
    def _():
        y_ref[...] = jnp.zeros_like(y_ref)


def _experts(xs, plan, w_gate, w_up, w_down, max_tiles):
    width = xs.shape[1]
    d = width - LANES
    hidden = w_gate.shape[-1]
    last = lambda t, te, nt: jnp.minimum(t, nt[0] - 1)
    return pl.pallas_call(
        functools.partial(_expert_kernel, d=d, hidden=hidden),
        grid_spec=pltpu.PrefetchScalarGridSpec(
            num_scalar_prefetch=2,
            grid=(max_tiles,),
            in_specs=[
                pl.BlockSpec((TE, width), lambda t, te, nt: (last(t, te, nt), 0)),
                pl.BlockSpec((None, d, hidden), lambda t, te, nt: (te[last(t, te, nt)], 0, 0)),
                pl.BlockSpec((None, d, hidden), lambda t, te, nt: (te[last(t, te, nt)], 0, 0)),
                pl.BlockSpec((None, hidden, d), lambda t, te, nt: (te[last(t, te, nt)], 0, 0)),
            ],
            out_specs=pl.BlockSpec((TE, d), lambda t, te, nt: (t, 0)),
            scratch_shapes=[
                pltpu.VMEM((d, 2 * hidden), BF16),
                pltpu.VMEM((hidden, d), BF16),
            ],
        ),
        out_shape=jax.ShapeDtypeStruct((max_tiles * TE, d), F32),
        compiler_params=_cparams(("arbitrary",)),
        name="experts",
    )(plan["tile_expert"], plan["n_tiles"], xs, w_gate, w_up, w_down)


def _combine_kernel(tile_units_ref, global_unit_ref, y_hbm, x1_ref, route_ref, loff_ref, mod_ref, lng_ref,
                    lnb_ref, o_ref, ybuf, sems, *, d, n_tok_tiles):
    t = pl.program_id(0)
    slot = t % 2

    def pull(s, wait):
        def fn(local_row, global_row):
            cp = pltpu.make_async_copy(y_hbm.at[pl.ds(global_row, CHUNK), :],
                                       ybuf.at[s, pl.ds(local_row, CHUNK), :], sems.at[s])
            cp.wait() if wait else cp.start()
        return fn

    @pl.when(t == 0)
    def _():
        ybuf[...] = jnp.zeros_like(ybuf)
        _for_each_unit(tile_units_ref, global_unit_ref, 0, pull(0, False))

    @pl.when(t + 1 < n_tok_tiles)
    def _():
        _for_each_unit(tile_units_ref, global_unit_ref, t + 1, pull(1 - slot, False))

    _for_each_unit(tile_units_ref, global_unit_ref, t, pull(slot, True))

    pos1, pos2 = _local_positions(route_ref[...], loff_ref[...])
    sel = (_selection(pos1) | _selection(pos2)).astype(BF16)
    y = ybuf[slot]
    y_hi = y.astype(BF16)
    y_lo = (y - y_hi.astype(F32)).astype(BF16)
    fx = jnp.dot(sel, y_hi, preferred_element_type=F32) + jnp.dot(sel, y_lo, preferred_element_type=F32)
    gate_f = mod_ref[:, 5 * d:6 * d]
    o_ref[...] = _layer_norm(DEEPNORM_ALPHA * x1_ref[...] + gate_f * fx, lng_ref[...], lnb_ref[...])


def _combine(ys, plan, x1, route, mod, ln_g, ln_b, mod_tile_of):
    rows, d = x1.shape
    n_tok_tiles = rows // TM
    const = lambda t, *_: (0, 0)
    return pl.pallas_call(
        functools.partial(_combine_kernel, d=d, n_tok_tiles=n_tok_tiles),
        grid_spec=pltpu.PrefetchScalarGridSpec(
            num_scalar_prefetch=2,
            grid=(n_tok_tiles,),
            in_specs=[
                pl.BlockSpec(memory_space=pl.ANY),
                pl.BlockSpec((TM, d), lambda t, *_: (t, 0)),
                pl.BlockSpec((TM, LANES), lambda t, *_: (t, 0)),
                pl.BlockSpec((None, 1, LANES), lambda t, *_: (t, 0, 0)),
                pl.BlockSpec((None, 1, mod.shape[-1]), lambda t, *_: mod_tile_of(t)),
                pl.BlockSpec((1, d), const),
                pl.BlockSpec((1, d), const),
            ],
            out_specs=pl.BlockSpec((TM, d), lambda t, *_: (t, 0)),
            scratch_shapes=[
                pltpu.VMEM((2, LOCAL_ROWS, d), F32),
                pltpu.SemaphoreType.DMA((2,)),
            ],
        ),
        out_shape=jax.ShapeDtypeStruct((rows, d), F32),
        compiler_params=_cparams(("arbitrary",)),
        name="combine",
    )(plan["tile_units"], plan["global_unit"], ys, x1, route, plan["local_off_rows"], mod, ln_g, ln_b)


def kernel(x, c, ctx, c_ctx, w_mod, b_mod, ln_mix_g, ln_mix_b, ln_ffn_g, ln_ffn_b, diff_w_qkv, diff_w_o, diff_lambda_q1, diff_lambda_k1, diff_lambda_q2, diff_lambda_k2, diff_subln_g, gqa_w_qkv, gqa_w_o, gqa_q_norm_g, gqa_k_norm_g, moe_w_group, moe_b_group, moe_w_router, moe_b_router, moe_w_gate, moe_w_up, moe_w_down):
    batch, n, d = x.shape
    n_ctx = ctx.shape[1]
    t = n_ctx + n
    assert n % TM == 0 and n_ctx % TM == 0 and n % GRID_W == 0 and d % LANES == 0
    assert w_mod.shape[0] == DEPTH
    tiles_per_sample = t // TM
    ctx_tiles = n_ctx // TM
    q_tiles = tiles_per_sample - ctx_tiles

    pad = (-(batch + 1)) % SUBLANES
    cond = jnp.concatenate([c, c_ctx[None, :], jnp.zeros((pad, d), F32)], axis=0)
    ctx_row = batch
    mod_all = _modulation(cond, w_mod, b_mod)
    mod_all = mod_all.reshape(DEPTH, cond.shape[0], 1, N_MOD * d)

    x_parts = (ctx.reshape(batch * n_ctx, d), x.reshape(batch * n, d))

    for i in range(DEPTH):
        last = i == DEPTH - 1
        mod = mod_all[i]
        j = i // 2
        lng_m, lnb_m = ln_mix_g[i][None, :], ln_mix_b[i][None, :]
        lng_f, lnb_f = ln_ffn_g[i][None, :], ln_ffn_b[i][None, :]
        if i % 2 == 0:
            lambda_init = 0.8 - 0.6 * math.exp(-0.3 * i)
            tab = _rope_tables(n_ctx, n, DIFF_HEAD_DIM)
            q, k, v = _qkv_proj("diff", x_parts, mod, diff_w_qkv[j].astype(BF16), tab, None,
                                tiles_per_sample, ctx_tiles, ctx_row)
            lam_vecs = jnp.stack([diff_lambda_q1[j], diff_lambda_k1[j], diff_lambda_q2[j], diff_lambda_k2[j]])
            o_parts = _diff_attention(q, k, v, lam_vecs.astype(F32), diff_subln_g[j][None, :], batch, t, n_ctx,
                                      lambda_init)
            if last:
                o_parts = o_parts[1:]
            wo = diff_w_o[j]
        else:
            tab = _rope_tables(n_ctx, n, GQA_HEAD_DIM)
            q, k, v = _qkv_proj("gqa", x_parts, mod, gqa_w_qkv[j].astype(BF16), tab,
                                (gqa_q_norm_g[j][None, :], gqa_k_norm_g[j][None, :]),
                                tiles_per_sample, ctx_tiles, ctx_row)
            if not last:
                raise NotImplementedError("grouped-query layer with context outputs")
            o_parts = (_gqa_attention(q, k, v, batch, t, n_ctx),)
            wo = gqa_w_o[j]

        w_r = jnp.zeros((d, LANES), F32)
        w_r = w_r.at[:, :N_EXPERTS].set(moe_w_router[i]).at[:, GROUP_LANE0:GROUP_LANE0 + N_GROUPS].set(moe_w_group[i])
        b_r = jnp.zeros((1, LANES), F32)
        b_r = b_r.at[0, :N_EXPERTS].set(moe_b_router[i]).at[0, GROUP_LANE0:GROUP_LANE0 + N_GROUPS].set(moe_b_group[i])
        wr_hi = w_r.astype(BF16)
        wr_lo = (w_r - wr_hi.astype(F32)).astype(BF16)

        x1, h2, route, counts = _proj_route(o_parts, x_parts, mod, wo.astype(BF16), lng_m, lnb_m, wr_hi, wr_lo,
                                            b_r, tiles_per_sample, ctx_tiles, ctx_row)
        n_tok_tiles = x1.shape[0] // TM
        max_tiles = (2 * n_tok_tiles * TM + n_tok_tiles * N_EXPERTS * (CHUNK - 1)) // TE + N_EXPERTS
        plan = _dispatch_plan(counts, max_tiles, expert_base=i * N_EXPERTS)
        xsorted = _dispatch(h2, route, plan, max_tiles)
        hidden = moe_w_gate.shape[-1]
        w_gate = moe_w_gate.reshape(DEPTH * N_EXPERTS, d, hidden)
        w_up = moe_w_up.reshape(DEPTH * N_EXPERTS, d, hidden)
        w_down = moe_w_down.reshape(DEPTH * N_EXPERTS, hidden, d)
        ys = _experts(xsorted, plan, w_gate, w_up, w_down, max_tiles)

        row_map = _mod_row_map(tiles_per_sample, ctx_tiles, ctx_row)
        if last:
            mod_tile_of = lambda tt: row_map((tt // q_tiles) * tiles_per_sample + ctx_tiles + tt % q_tiles)
        else:
            mod_tile_of = row_map
        x_parts = (_combine(ys, plan, x1, route, mod, lng_f, lnb_f, mod_tile_of),)

    return x_parts[0].reshape(batch, n, d)
```
